```python
import math
import jax
import jax.numpy as jnp
from jax import lax
import numpy as np

D_MODEL = 1024
BATCH = 16
SEQ = 256
DEPTH = 4
DEC_BATCH = 2
DEC_SEQ = 2048
PAST_LEN = 512

GRID_W = 64
HG_HEADS = 4
HG_DK = 64
HG_DV = 64
HG_QK = HG_HEADS * HG_DK
HG_W = HG_HEADS * HG_DV
DA_HEADS = 4
DA_DQK = 64
DA_DV = 2 * DA_DQK
DA_QK = 2 * DA_HEADS * DA_DQK
DA_W = DA_HEADS * DA_DV
HY_W = D_MODEL - HG_W - DA_W
HY_ORDER = 2
HY_SHORT = 3
HY_POS_EMB = 33
HY_HIDDEN = 64
HY_SHORT_DECAY_PCT = 0.3
HY_LONG_DECAY_PCT = 1.5
HY_TARGET = 1e-2
N_EXPERTS = 16
EC_CAPACITY_FACTOR = 2
D_FF_EXPERT = 1024
N_MOD = 6
CHUNK = 32
Q_BLOCK = 128
ROPE_BASE = 10000.0
NORM_EPS = 1e-6
IN_SIZES = (HG_QK, HG_QK, HG_QK, HG_W, HG_W, DA_QK, DA_QK, DA_W, (HY_ORDER + 1) * HY_W)
D_IN = 3 * HG_QK + 2 * HG_W + 2 * DA_QK + DA_W + (HY_ORDER + 1) * HY_W

kernel_name = 'hybrid_diffusion_step'

F32 = jnp.float32


def _rmsnorm(x, g):
    x32 = x.astype(F32)
    return x32 * lax.rsqrt(jnp.mean(x32 * x32, axis=-1, keepdims=True) + NORM_EPS) * g.astype(F32)


def _split_cols(z):
    out = []
    start = 0
    for size in IN_SIZES:
        out.append(z[..., start:start + size])
        start += size
    return out


def _modulation(cond, w_mod, b_mod):
    m = jax.nn.silu(cond.astype(F32)) @ w_mod.astype(F32) + b_mod.astype(F32)
    return jnp.split(m[:, None, :], N_MOD, axis=-1)


def _gla_chunk_scan(q, k, v, logf, s0):
    n, h, L, _ = q.shape
    dv = v.shape[-1]
    nc = L // CHUNK

    def chunks(a):
        return jnp.moveaxis(a.reshape(n, h, nc, CHUNK, a.shape[-1]), 2, 0)

    lower = jnp.tril(jnp.ones((CHUNK, CHUNK), dtype=bool))[:, :, None]

    def step(state, blk):
        qc, kc, vc, gc = blk
        b = jnp.cumsum(gc, axis=2)
        o_inter = jnp.einsum('nhtk,nhkv->nhtv', qc * jnp.exp(b), state)
        rel = jnp.where(lower, b[:, :, :, None, :] - b[:, :, None, :, :], -jnp.inf)
        att = jnp.einsum('nhtk,nhtsk,nhsk->nhts', qc, jnp.exp(rel), kc)
        o = o_inter + jnp.einsum('nhts,nhsv->nhtv', att, vc)
        b_end = b[:, :, -1:, :]
        state = (jnp.exp(b_end[:, :, 0, :])[..., None] * state
                 + jnp.einsum('nhsk,nhsv->nhkv', kc * jnp.exp(b_end - b), vc))
        return state, o

    s_fin, o = lax.scan(step, s0.astype(F32), (chunks(q), chunks(k), chunks(v), chunks(logf)))
    o = jnp.moveaxis(o, 0, 2).reshape(n, h, L, dv)
    return o, s_fin


def _hgrn_mixer(hq, hff, hfb, hi, hg, lb, s0, norm_g):
    B, L, _ = hq.shape

    def heads(a, d):
        return a.reshape(B, L, HG_HEADS, d).transpose(0, 2, 1, 3)

    def gates(f_logit, lbd):
        f = heads(f_logit, HG_DK)
        lbd = lbd.reshape(HG_HEADS, 1, HG_DK)
        k = (1.0 - lbd) * jax.nn.sigmoid(-f)
        logf = jnp.logaddexp(jnp.log(lbd), jnp.log1p(-lbd) + jax.nn.log_sigmoid(f))
        return k, logf

    q = heads(hq, HG_DK)
    v = heads(hi, HG_DV)
    k_f, g_f = gates(hff, lb[0])
    k_b, g_b = gates(hfb, lb[1])

    def rev(a):
        return a[:, :, ::-1]

    o, s_fin = _gla_chunk_scan(
        jnp.concatenate([q, rev(q)], axis=0),
        jnp.concatenate([k_f, rev(k_b)], axis=0),
        jnp.concatenate([v, rev(v)], axis=0),
        jnp.concatenate([g_f, rev(g_b)], axis=0),
        s0.reshape(2 * B, HG_HEADS, HG_DK, HG_DV))
    o = o[:B] + rev(o[B:])
    o = _rmsnorm(o, norm_g).transpose(0, 2, 1, 3).reshape(B, L, HG_W)
    return o * jax.nn.silu(hg), s_fin.reshape(2, B, HG_HEADS, HG_DK, HG_DV)


def _axial_rope(x):
    L = x.shape[-2]
    n_rows = L // GRID_W
    rows, cols = jnp.meshgrid(jnp.arange(n_rows), jnp.arange(GRID_W), indexing='ij')
    half = DA_DQK // 2
    inv = ROPE_BASE ** (-jnp.arange(0, half, 2, dtype=F32) / half)

    def rot(xa, pos):
        ang = pos.reshape(-1).astype(F32)[:, None] * inv[None, :]
        ang = jnp.concatenate([ang, ang], axis=-1)
        x1, x2 = jnp.split(xa, 2, axis=-1)
        return xa * jnp.cos(ang) + jnp.concatenate([-x2, x1], axis=-1) * jnp.sin(ang)

    return jnp.concatenate([rot(x[..., :half], rows), rot(x[..., half:], cols)], axis=-1)


def _diff_attention(q, k, v, lam):
    B, H, _, Lq, _ = q.shape
    nb = Lq // Q_BLOCK
    qb = jnp.moveaxis(q.reshape(B, H, 2, nb, Q_BLOCK, DA_DQK), 3, 0)
    scale = DA_DQK ** -0.5

    def block(qblk):
        s = jnp.einsum('bhmqd,bhmkd->bhmqk', qblk, k) * scale
        p = jax.nn.softmax(s, axis=-1)
        a = p[:, :, 0] - lam * p[:, :, 1]
        return jnp.einsum('bhqk,bhkv->bhqv', a, v)

    o = lax.map(block, qb)
    return jnp.moveaxis(o, 0, 2).reshape(B, H, Lq, DA_DV)


def _hyena_filters(L, w1, b1, w2, b2, w3, freq):
    t = jnp.linspace(0.0, 1.0, L, dtype=F32)[:, None]
    bands = (HY_POS_EMB - 1) // 2
    fr = jnp.linspace(1e-4, bands - 1, bands, dtype=F32)[None, :]
    w = 2.0 * math.pi * jnp.arange(L, dtype=F32)[:, None] / L
    z = jnp.concatenate([t, jnp.cos(fr * w), -jnp.sin(fr * w)], axis=-1)
    hid = jnp.sin(freq[0].astype(F32) * (z @ w1.astype(F32) + b1.astype(F32)))
    hid = jnp.sin(freq[1].astype(F32) * (hid @ w2.astype(F32) + b2.astype(F32)))
    h = (hid @ w3.astype(F32)).reshape(L, 2, HY_ORDER, HY_W)
    max_decay = math.log(HY_TARGET) / HY_SHORT_DECAY_PCT
    min_decay = math.log(HY_TARGET) / HY_LONG_DECAY_PCT
    deltas = jnp.linspace(min_decay, max_decay, HY_W, dtype=F32)
    h = h * jnp.exp(-t[:, :, None, None] * jnp.abs(deltas))
    h = h / jnp.sum(jnp.abs(h), axis=(0, 1), keepdims=True)
    fwd, bwd = h[:, 0], h[:, 1]
    circ_filter = jnp.concatenate([fwd, jnp.zeros_like(fwd[:1]), bwd[:0:-1]], axis=0)
    return jnp.fft.rfft(circ_filter, axis=0)


def _long_conv(u, filt_f, skip):
    L = u.shape[1]
    uf = jnp.fft.rfft(u, n=2 * L, axis=1)
    y = jnp.fft.irfft(uf * filt_f[None], n=2 * L, axis=1)[:, :L]
    return y + u * skip


def _hyena_mixer(zh, filt_f, skip, conv_w, conv_b):
    L = zh.shape[1]
    pad = HY_SHORT // 2
    zp = jnp.pad(zh, ((0, 0), (pad, pad), (0, 0)))
    zc = conv_b.astype(F32)
    for j in range(HY_SHORT):
        zc = zc + zp[:, j:j + L] * conv_w[j].astype(F32)
    v, x1, x2 = jnp.split(zc, HY_ORDER + 1, axis=-1)
    z = x1 * _long_conv(v, filt_f[:, 0], skip[0].astype(F32))
    return x2 * _long_conv(z, filt_f[:, 1], skip[1].astype(F32))


def _ec_ffn(h, w_router, w_gate, w_up, w_down):
    B, T, D = h.shape
    cap = EC_CAPACITY_FACTOR * T // N_EXPERTS
    aff = jax.nn.softmax(jnp.einsum('btd,de->bte', h, w_router.astype(F32)), axis=-1)
    gate, idx = lax.top_k(jnp.swapaxes(aff, 1, 2), cap)
    xs = jax.vmap(lambda hb, ib: hb[ib])(h, idx)
    hid = (jax.nn.silu(jnp.einsum('becd,edf->becf', xs, w_gate.astype(F32)))
           * jnp.einsum('becd,edf->becf', xs, w_up.astype(F32)))
    out = jnp.einsum('becf,efd->becd', hid, w_down.astype(F32)) * gate[..., None]
    return jax.vmap(lambda ib, ob: jnp.zeros((T, D), ob.dtype).at[ib.reshape(-1)].add(ob.reshape(-1, D)))(idx, out)


def _trunk_layer(x, cond, p, layer_idx, ctx_k, ctx_v, s0):
    is_latent = ctx_k is not None
    B, L, _ = x.shape
    sh1, sc1, g1, sh2, sc2, g2 = _modulation(cond, p['w_mod'], p['b_mod'])
    h = _rmsnorm(x, p['norm_g'][0]) * (1.0 + sc1) + sh1
    z = jnp.einsum('bld,de->ble', h, p['w_in'].astype(F32))
    hq, hff, hfb, hi, hg, aq, ak, av, hy = _split_cols(z)

    if s0 is None:
        s0 = jnp.zeros((2, B, HG_HEADS, HG_DK, HG_DV), F32)
    o_hg, s_fin = _hgrn_mixer(hq, hff, hfb, hi, hg, p['lb'], s0.astype(F32), p['hgrn_norm_g'])

    qh = aq.reshape(B, L, DA_HEADS, 2, DA_DQK).transpose(0, 2, 3, 1, 4)
    kh = ak.reshape(B, L, DA_HEADS, 2, DA_DQK).transpose(0, 2, 3, 1, 4)
    vh = av.reshape(B, L, DA_HEADS, DA_DV).transpose(0, 2, 1, 3)
    if is_latent:
        qh = _axial_rope(qh)
        kh = _axial_rope(kh)
        keys = jnp.concatenate([kh, ctx_k.astype(F32)], axis=3)
        vals = jnp.concatenate([vh, ctx_v.astype(F32)], axis=2)
    else:
        keys, vals = kh, vh
    lam_init = 0.8 - 0.6 * math.exp(-0.3 * layer_idx)
    lv = p['diff_lambda'].astype(F32)
    lam = jnp.exp(jnp.sum(lv[0] * lv[1])) - jnp.exp(jnp.sum(lv[2] * lv[3])) + lam_init
    o_da = _diff_attention(qh, keys, vals, lam)
    o_da = _rmsnorm(o_da, p['diff_subln_g']) * (1.0 - lam_init)
    o_da = o_da.transpose(0, 2, 1, 3).reshape(B, L, DA_W)

    filt_f = _hyena_filters(L, p['hy_filt_w1'], p['hy_filt_b1'], p['hy_filt_w2'], p['hy_filt_b2'],
                            p['hy_filt_w3'], p['hy_filt_freq'])
    o_hy = _hyena_mixer(hy, filt_f, p['hy_skip'], p['hy_conv_w'], p['hy_conv_b'])

    mix = jnp.concatenate([o_hg, o_da, o_hy], axis=-1) @ p['w_out'].astype(F32)
    x = x + g1 * mix
    h2 = _rmsnorm(x, p['norm_g'][1]) * (1.0 + sc2) + sh2
    x = x + g2 * _ec_ffn(h2, p['w_router'], p['w_gate'], p['w_up'], p['w_down'])
    return x, kh, vh, s_fin


def setup_inputs(seed: int = 0) -> dict:
    key = jax.random.key(seed)
    ks = jax.random.split(key, 32)

    def nrm(k, shape, s):
        return jax.random.normal(k, shape, F32) * s

    return {
        'x_prompt': nrm(ks[0], (BATCH, SEQ, D_MODEL), 1.0),
        'x_sample': nrm(ks[1], (DEC_BATCH, DEC_SEQ, D_MODEL), 1.0),
        'c': nrm(ks[2], (DEC_BATCH, D_MODEL), 1.0),
        'cache_diff_k': nrm(ks[3], (DEC_BATCH, DEPTH, DA_HEADS, 2, PAST_LEN, DA_DQK), 1.0),
        'cache_diff_v': nrm(ks[4], (DEC_BATCH, DEPTH, DA_HEADS, PAST_LEN, DA_DV), 1.0),
        'state_hgrn': nrm(ks[5], (DEC_BATCH, DEPTH, 2, HG_HEADS, HG_DK, HG_DV), 0.5),
        'c_ctx': nrm(ks[6], (D_MODEL,), 1.0),
        'norm_g': 1.0 + nrm(ks[7], (DEPTH, 2, D_MODEL), 0.02),
        'final_norm_g': 1.0 + nrm(ks[8], (D_MODEL,), 0.02),
        'w_mod': nrm(ks[9], (DEPTH, D_MODEL, N_MOD * D_MODEL), 0.5 * D_MODEL ** -0.5),
        'b_mod': nrm(ks[10], (DEPTH, N_MOD * D_MODEL), 0.02),
        'w_in': nrm(ks[11], (DEPTH, D_MODEL, D_IN), D_MODEL ** -0.5),
        'w_out': nrm(ks[12], (DEPTH, D_MODEL, D_MODEL), D_MODEL ** -0.5),
        'hgrn_lb_logits': nrm(ks[13], (DEPTH, 2, HG_QK), 0.5),
        'hgrn_norm_g': 1.0 + nrm(ks[14], (DEPTH, HG_DV), 0.02),
        'diff_lambda': nrm(ks[15], (DEPTH, 4, DA_DQK), 0.1),
        'diff_subln_g': 1.0 + nrm(ks[16], (DEPTH, DA_DV), 0.02),
        'hy_conv_w': nrm(ks[17], (DEPTH, HY_SHORT, (HY_ORDER + 1) * HY_W), 0.5),
        'hy_conv_b': nrm(ks[18], (DEPTH, (HY_ORDER + 1) * HY_W), 0.02),
        'hy_filt_w1': nrm(ks[19], (DEPTH, HY_POS_EMB, HY_HIDDEN), HY_POS_EMB ** -0.5),
        'hy_filt_b1': nrm(ks[20], (DEPTH, HY_HIDDEN), 0.02),
        'hy_filt_w2': nrm(ks[21], (DEPTH, HY_HIDDEN, HY_HIDDEN), HY_HIDDEN ** -0.5),
        'hy_filt_b2': nrm(ks[22], (DEPTH, HY_HIDDEN), 0.02),
        'hy_filt_w3': nrm(ks[23], (DEPTH, HY_HIDDEN, 2 * HY_ORDER * HY_W), HY_HIDDEN ** -0.5),
        'hy_filt_freq': 1.0 + nrm(ks[24], (DEPTH, 2, HY_HIDDEN), 0.1),
        'hy_skip': nrm(ks[25], (DEPTH, HY_ORDER, HY_W), 0.5),
        'w_router': nrm(ks[26], (DEPTH, D_MODEL, N_EXPERTS), D_MODEL ** -0.5),
        'w_gate': nrm(ks[27], (DEPTH, N_EXPERTS, D_MODEL, D_FF_EXPERT), D_MODEL ** -0.5),
        'w_up': nrm(ks[28], (DEPTH, N_EXPERTS, D_MODEL, D_FF_EXPERT), D_MODEL ** -0.5),
        'w_down': nrm(ks[29], (DEPTH, N_EXPERTS, D_FF_EXPERT, D_MODEL), D_FF_EXPERT ** -0.5),
    }


def reference(x_prompt, x_sample, c, cache_diff_k, cache_diff_v, state_hgrn, c_ctx, norm_g, final_norm_g,
              w_mod, b_mod, w_in, w_out, hgrn_lb_logits, hgrn_norm_g, diff_lambda, diff_subln_g,
              hy_conv_w, hy_conv_b, hy_filt_w1, hy_filt_b1, hy_filt_w2, hy_filt_b2, hy_filt_w3, hy_filt_freq,
              hy_skip, w_router, w_gate, w_up, w_down):
    lb_cum = jnp.cumsum(jax.nn.softmax(hgrn_lb_logits.astype(F32), axis=0), axis=0)
    lb_all = lb_cum - lb_cum[:1]

    xp = x_prompt.astype(F32)
    xs = x_sample.astype(F32)
    cond_ctx = c_ctx.astype(F32)[None, :]
    cond_lat = c.astype(F32)
    new_k, new_v, new_s = [], [], []
    for l in range(DEPTH):
        p = {
            'norm_g': norm_g[l], 'w_mod': w_mod[l], 'b_mod': b_mod[l], 'w_in': w_in[l], 'w_out': w_out[l],
            'lb': lb_all[l], 'hgrn_norm_g': hgrn_norm_g[l], 'diff_lambda': diff_lambda[l],
            'diff_subln_g': diff_subln_g[l], 'hy_conv_w': hy_conv_w[l], 'hy_conv_b': hy_conv_b[l],
            'hy_filt_w1': hy_filt_w1[l], 'hy_filt_b1': hy_filt_b1[l], 'hy_filt_w2': hy_filt_w2[l],
            'hy_filt_b2': hy_filt_b2[l], 'hy_filt_w3': hy_filt_w3[l], 'hy_filt_freq': hy_filt_freq[l],
            'hy_skip': hy_skip[l], 'w_router': w_router[l], 'w_gate': w_gate[l], 'w_up': w_up[l],
            'w_down': w_down[l],
        }
        xp, k_ctx, v_ctx, s_ctx = _trunk_layer(xp, cond_ctx, p, l, None, None, None)
        new_k.append(k_ctx)
        new_v.append(v_ctx)
        new_s.append(jnp.swapaxes(s_ctx, 0, 1))
        xs, _, _, _ = _trunk_layer(xs, cond_lat, p, l, cache_diff_k[:, l], cache_diff_v[:, l],
                                   jnp.swapaxes(state_hgrn[:, l], 0, 1))

    y_prompt = _rmsnorm(xp, final_norm_g).astype(x_prompt.dtype)
    y_sample = _rmsnorm(xs, final_norm_g).astype(x_sample.dtype)
    new_cache_diff_k = jnp.stack(new_k, axis=1).astype(x_prompt.dtype)
    new_cache_diff_v = jnp.stack(new_v, axis=1).astype(x_prompt.dtype)
    new_state_hgrn = jnp.stack(new_s, axis=1).astype(x_prompt.dtype)
    return (y_prompt, y_sample, new_cache_diff_k, new_cache_diff_v, new_state_hgrn)
```

```python
import functools
import math

import jax
import jax.numpy as jnp
import numpy as np
from jax import lax
from jax.experimental import pallas as pl
from jax.experimental.pallas import tpu as pltpu

F32 = jnp.float32
BF16 = jnp.bfloat16
HI = lax.Precision.HIGHEST

D_MODEL = 1024
DEPTH = 4
GRID_W = 64
HG_HEADS = 4
HG_DK = 64
HG_W = 256
DA_HEADS = 4
DA_DQK = 64
DA_DV = 128
DA_W = 512
HY_W = 256
HY_ORDER = 2
HY_POS_EMB = 33
HY_HIDDEN = 64
HY_SHORT_DECAY_PCT = 0.3
HY_LONG_DECAY_PCT = 1.5
HY_TARGET = 1e-2
N_EXPERTS = 16
EC_CAPACITY_FACTOR = 2
D_FF = 1024
N_MOD = 6
ROPE_BASE = 10000.0
NORM_EPS = 1e-6
D_IN = 3584
COL_HQ, COL_HFF, COL_HFB, COL_HI, COL_HG = 0, 256, 512, 768, 1024
COL_AQ, COL_AK, COL_AV, COL_HY = 1280, 1792, 2304, 2816

LANES = 128
HGRN_CHUNK = 16
VMEM_LIMIT = 56 * 1024 * 1024


def _cparams(n_axes):
    return pltpu.CompilerParams(dimension_semantics=("arbitrary",) * n_axes,
                                vmem_limit_bytes=VMEM_LIMIT)


def _dot(a, b):
    return jnp.dot(a.astype(BF16), b.astype(BF16), preferred_element_type=F32)


def _dot_nt(a, b):
    return lax.dot_general(a.astype(BF16), b.astype(BF16), (((1,), (1,)), ((), ())),
                           preferred_element_type=F32)


def _dot_tn(a, b):
    return lax.dot_general(a.astype(BF16), b.astype(BF16), (((0,), (0,)), ((), ())),
                           preferred_element_type=F32)


def _dot_hi(a, b):
    return jnp.dot(a, b, precision=HI, preferred_element_type=F32)


def _sigmoid(x):
    return 1.0 / (1.0 + jnp.exp(-x))


def _silu(x):
    return x * _sigmoid(x)


def _mod_kernel(cond_ref, w_ref, b_ref, o_ref):
    s = _silu(cond_ref[...])
    o_ref[...] = _dot(s, w_ref[...]) + b_ref[...]


def _modulation(cond8, w_mod, b_mod):
    tn = 1536
    n_mod = N_MOD * D_MODEL
    return pl.pallas_call(
        _mod_kernel,
        out_shape=jax.ShapeDtypeStruct((DEPTH, 8, n_mod), F32),
        grid=(DEPTH, n_mod // tn),
        in_specs=[
            pl.BlockSpec((8, D_MODEL), lambda l, j: (0, 0)),
            pl.BlockSpec((None, D_MODEL, tn), lambda l, j: (l, 0, j)),
            pl.BlockSpec((None, 1, tn), lambda l, j: (l, 0, j)),
        ],
        out_specs=pl.BlockSpec((None, 8, tn), lambda l, j: (l, 0, j)),
        compiler_params=_cparams(2),
        name="modulation",
    )(cond8, w_mod, b_mod.reshape(DEPTH, 1, n_mod))


def _rms(x):
    return x * lax.rsqrt(jnp.mean(x * x, axis=-1, keepdims=True) + NORM_EPS)


def _in_kernel(cond_base, cond_stride, x_ref, mod_ref, g_ref, w_ref, z_ref):
    row = cond_base + cond_stride * pl.program_id(0)
    shift = mod_ref[pl.ds(row, 1), pl.ds(0, D_MODEL)]
    scale = mod_ref[pl.ds(row, 1), pl.ds(D_MODEL, D_MODEL)]
    h = _rms(x_ref[...]) * g_ref[...] * (1.0 + scale) + shift
    z_ref[...] = _dot(h, w_ref[...])


def _in_proj(x, mod_l, g, w_bf, cond_base, cond_stride):
    B, L, _ = x.shape
    tm = 256
    return pl.pallas_call(
        functools.partial(_in_kernel, cond_base, cond_stride),
        out_shape=jax.ShapeDtypeStruct((B, L, D_IN), F32),
        grid=(B, L // tm),
        in_specs=[
            pl.BlockSpec((None, tm, D_MODEL), lambda b, i: (b, i, 0)),
            pl.BlockSpec((8, N_MOD * D_MODEL), lambda b, i: (0, 0)),
            pl.BlockSpec((1, D_MODEL), lambda b, i: (0, 0)),
            pl.BlockSpec((D_MODEL, D_IN), lambda b, i: (0, 0)),
        ],
        out_specs=pl.BlockSpec((None, tm, D_IN), lambda b, i: (b, i, 0)),
        compiler_params=_cparams(2),
        name="in_proj",
    )(x, mod_l, g, w_bf)


def _head_block_mask(n, width):
    r = lax.broadcasted_iota(jnp.int32, (n, n), 0) // width
    c = lax.broadcasted_iota(jnp.int32, (n, n), 1) // width
    return r == c


def _hgrn_kernel(L, layer, q_ref, ff_ref, fb_ref, v_ref, g_ref, lbl_ref, s0_ref, ng_ref,
                 o_ref, sfin_ref, bf_s, bb_s, kf_s, kb_s, of_s, ob_s, sf_s, sb_s):
    C = HGRN_CHUNK
    W = HG_W
    nc = L // C

    lg = [lbl_ref[j] for j in range(DEPTH)]
    mx = functools.reduce(jnp.maximum, lg)
    ex = [jnp.exp(a - mx) for a in lg]
    den = functools.reduce(lambda a, b: a + b, ex)
    sm = [e / den for e in ex]
    cum = sm[0]
    for j in range(1, layer + 1):
        cum = cum + sm[j]
    lb = cum - sm[0]

    rowc = lax.broadcasted_iota(jnp.int32, (L, W), 0) & (C - 1)

    def gates(f, lbd):
        log_sig = jnp.minimum(f, 0.0) - jnp.log1p(jnp.exp(-jnp.abs(f)))
        a = jnp.log(lbd)
        c = jnp.log1p(-lbd) + log_sig
        logf = jnp.maximum(a, c) + jnp.log1p(jnp.exp(-jnp.abs(a - c)))
        k = (1.0 - lbd) * _sigmoid(-f)
        return k, logf

    kf, lf = gates(ff_ref[...], lb[0:1])
    kb, lbk = gates(fb_ref[...], lb[1:2])
    kf_s[...] = kf
    kb_s[...] = kb
    d = 1
    while d < C:
        lf = lf + jnp.where(rowc >= d, pltpu.roll(lf, d, 0), 0.0)
        lbk = lbk + jnp.where(rowc < C - d, pltpu.roll(lbk, L - d, 0), 0.0)
        d *= 2
    bf_s[...] = lf
    bb_s[...] = lbk

    sf_s[...] = s0_ref[0]
    sb_s[...] = s0_ref[1]
    bd = _head_block_mask(W, HG_DK)
    bd_bf = jnp.where(bd, 1.0, 0.0).astype(BF16)
    trow = lax.broadcasted_iota(jnp.int32, (C, W), 0)

    def chunk(r0, b_s, k_s, o_s, st_s, forward):
        sl = pl.ds(r0, C)
        q = q_ref[sl, :]
        v = v_ref[sl, :]
        b = b_s[sl, :]
        k = k_s[sl, :]
        st = st_s[...]
        acc = _dot_nt(q * jnp.exp(b), st)
        for s in range(C):
            keep = (trow >= s) if forward else (trow <= s)
            p = jnp.where(keep, jnp.exp(b - b[s:s + 1]), 0.0) * q * k[s:s + 1]
            acc = acc + jnp.dot(p.astype(BF16), bd_bf, preferred_element_type=F32) * v[s:s + 1]
        o_s[sl, :] = acc
        b_end = b[C - 1:C] if forward else b[0:1]
        kv = _dot_tn(v, k * jnp.exp(b_end - b))
        st_s[...] = st * jnp.exp(b_end) + jnp.where(bd, kv, 0.0)

    def body(c, carry):
        chunk(pl.multiple_of(c * C, C), bf_s, kf_s, of_s, sf_s, True)
        chunk(pl.multiple_of((nc - 1 - c) * C, C), bb_s, kb_s, ob_s, sb_s, False)
        return carry

    lax.fori_loop(0, nc, body, 0)

    o = of_s[...] + ob_s[...]
    ms = _dot_hi(o * o, jnp.where(bd, 1.0 / HG_DK, 0.0).astype(F32))
    o_ref[...] = o * lax.rsqrt(ms + NORM_EPS) * ng_ref[...] * _silu(g_ref[...])
    sfin_ref[0] = sf_s[...]
    sfin_ref[1] = sb_s[...]


def _hgrn(z, lb_logits, s0t, ng_tiled, layer):
    B, L, _ = z.shape
    W = HG_W

    def col(c):
        return pl.BlockSpec((None, L, W), lambda b, c=c: (b, 0, c // W))

    seq = pltpu.VMEM((L, W), F32)
    st = pltpu.VMEM((W, W), F32)
    return pl.pallas_call(
        functools.partial(_hgrn_kernel, L, layer),
        out_shape=(jax.ShapeDtypeStruct((B, L, W), F32),
                   jax.ShapeDtypeStruct((B, 2, W, W), F32)),
        grid=(B,),
        in_specs=[col(COL_HQ), col(COL_HFF), col(COL_HFB), col(COL_HI), col(COL_HG),
                  pl.BlockSpec((DEPTH, 2, W), lambda b: (0, 0, 0)),
                  pl.BlockSpec((None, 2, W, W), lambda b: (b, 0, 0, 0)),
                  pl.BlockSpec((1, W), lambda b: (0, 0))],
        out_specs=(pl.BlockSpec((None, L, W), lambda b: (b, 0, 0)),
                   pl.BlockSpec((None, 2, W, W), lambda b: (b, 0, 0, 0))),
        scratch_shapes=[seq, seq, seq, seq, seq, seq, st, st],
        compiler_params=_cparams(1),
        name="hgrn",
    )(z, z, z, z, z, lb_logits, s0t, ng_tiled)


def _rope(x, cos, sin):
    lane = lax.broadcasted_iota(jnp.int32, x.shape, 1)
    swapped = jnp.where((lane & 31) < 16, -pltpu.roll(x, LANES - 16, 1), pltpu.roll(x, 16, 1))
    return x * cos + swapped * sin


def _attn_kernel(latent, lam_init, q_ref, k_ref, v_ref, *rest):
    if latent:
        ck_ref, cv_ref, cq_ref, sq_ref, ckk_ref, skk_ref, lam_ref, sg_ref, o_ref = rest
    else:
        lam_ref, sg_ref, o_ref = rest
    lv = lam_ref[...]
    lam = (jnp.exp(jnp.sum(lv[0:1] * lv[1:2], axis=1, keepdims=True))
           - jnp.exp(jnp.sum(lv[2:3] * lv[3:4], axis=1, keepdims=True)) + lam_init)
    q = q_ref[...]
    k = k_ref[...]
    v = v_ref[...]
    if latent:
        q = _rope(q, cq_ref[...], sq_ref[...])
        k = _rope(k, ckk_ref[...], skk_ref[...])
        cv = cv_ref[...]
    q = q * (DA_DQK ** -0.5)
    lane = lax.broadcasted_iota(jnp.int32, q.shape, 1)
    outs = []
    for m in range(2):
        qm = jnp.where((lane >= m * DA_DQK) & (lane < (m + 1) * DA_DQK), q, 0.0)
        s = _dot_nt(qm, k)
        mx = jnp.max(s, axis=1, keepdims=True)
        if latent:
            sc = _dot_nt(q[:, m * DA_DQK:(m + 1) * DA_DQK], ck_ref[m])
            mx = jnp.maximum(mx, jnp.max(sc, axis=1, keepdims=True))
        p = jnp.exp(s - mx)
        den = jnp.sum(p, axis=1, keepdims=True)
        o = _dot(p, v)
        if latent:
            pc = jnp.exp(sc - mx)
            den = den + jnp.sum(pc, axis=1, keepdims=True)
            o = o + _dot(pc, cv)
        outs.append(o / den)
    o = outs[0] - lam * outs[1]
    o_ref[...] = _rms(o) * sg_ref[...] * (1.0 - lam_init)


def _attention(z, layer, lam_p, subln_g, ctx_k=None, ctx_v=None, rope_tabs=None):
    B, L, _ = z.shape
    latent = ctx_k is not None
    tq = 256
    lam_init = 0.8 - 0.6 * math.exp(-0.3 * layer)
    qb, kb, vb = COL_AQ // LANES, COL_AK // LANES, COL_AV // LANES
    in_specs = [
        pl.BlockSpec((None, tq, LANES), lambda b, h, i: (b, i, qb + h)),
        pl.BlockSpec((None, L, LANES), lambda b, h, i: (b, 0, kb + h)),
        pl.BlockSpec((None, L, LANES), lambda b, h, i: (b, 0, vb + h)),
    ]
    args = [z, z, z]
    if latent:
        P = ctx_k.shape[4]
        cos_t, sin_t = rope_tabs
        in_specs += [
            pl.BlockSpec((None, None, None, 2, P, DA_DQK), lambda b, h, i: (b, layer, h, 0, 0, 0)),
            pl.BlockSpec((None, None, None, P, DA_DV), lambda b, h, i: (b, layer, h, 0, 0)),
            pl.BlockSpec((tq, LANES), lambda b, h, i: (i, 0)),
            pl.BlockSpec((tq, LANES), lambda b, h, i: (i, 0)),
            pl.BlockSpec((L, LANES), lambda b, h, i: (0, 0)),
            pl.BlockSpec((L, LANES), lambda b, h, i: (0, 0)),
        ]
        args += [ctx_k, ctx_v, cos_t, sin_t, cos_t, sin_t]
    in_specs += [pl.BlockSpec((4, DA_DQK), lambda b, h, i: (0, 0)),
                 pl.BlockSpec((1, DA_DV), lambda b, h, i: (0, 0))]
    args += [lam_p, subln_g]
    return pl.pallas_call(
        functools.partial(_attn_kernel, latent, lam_init),
        out_shape=jax.ShapeDtypeStruct((B, L, DA_W), F32),
        grid=(B, DA_HEADS, L // tq),
        in_specs=in_specs,
        out_specs=pl.BlockSpec((None, tq, DA_DV), lambda b, h, i: (b, i, h)),
        compiler_params=_cparams(3),
        name="diff_attention",
    )(*args)


def _rope_tables(L):
    half = DA_DQK // 2
    inv = ROPE_BASE ** (-jnp.arange(0, half, 2, dtype=F32) / half)
    t = jnp.arange(L)
    rows = (t // GRID_W).astype(F32)[:, None] * inv[None, :]
    cols = (t % GRID_W).astype(F32)[:, None] * inv[None, :]
    ang = jnp.concatenate([rows, rows, cols, cols], axis=-1)
    ang = jnp.concatenate([ang, ang], axis=-1)
    return jnp.cos(ang), jnp.sin(ang)


@functools.lru_cache(maxsize=None)
def _dft_np(L):
    idx = (np.arange(L, dtype=np.int64)[:, None] * np.arange(L, dtype=np.int64)[None, :]) % (2 * L)
    ang = idx.astype(np.float64) * (math.pi / L)
    return np.cos(ang).astype(np.float32), np.sin(ang).astype(np.float32)


def _dft_mats(L):
    c, s = _dft_np(L)
    return jnp.asarray(c).astype(BF16), jnp.asarray(s).astype(BF16)


def _filt_mlp_kernel(L, z_ref, w1_ref, b1_ref, w2_ref, b2_ref, w3f_ref, w3b_ref, fr_ref, dl_ref,
                     sum_ref, diff_ref, nyq_ref):
    z = z_ref[...]
    fr = fr_ref[...]
    hid = jnp.sin(fr[0:1] * (_dot_hi(z, w1_ref[...]) + b1_ref[...]))
    hid = jnp.sin(fr[1:2] * (_dot_hi(hid, w2_ref[...]) + b2_ref[...]))
    decay = jnp.exp(-z[:, 0:1] * jnp.abs(dl_ref[...]))
    hf = _dot_hi(hid, w3f_ref[...]) * decay
    hb = _dot_hi(hid, w3b_ref[...]) * decay
    nrm = (jnp.sum(jnp.abs(hf), axis=0, keepdims=True)
           + jnp.sum(jnp.abs(hb), axis=0, keepdims=True))
    row = lax.broadcasted_iota(jnp.int32, hf.shape, 0)
    hf = hf / nrm
    hb = jnp.where(row == 0, 0.0, hb / nrm)
    fs = hf + hb
    sum_ref[...] = fs
    diff_ref[...] = hf - hb
    sgn = jnp.where((row & 1) == 0, 1.0, -1.0)
    nyq_ref[...] = jnp.broadcast_to(jnp.sum(fs * sgn, axis=0, keepdims=True), nyq_ref.shape)


def _filt_spec_kernel(c_ref, s_ref, sum_ref, diff_ref, nyq_ref, a_ref, b_ref, d_ref):
    a = _dot(c_ref[...], sum_ref[...])
    him = -_dot(s_ref[...], diff_ref[...])
    row = lax.broadcasted_iota(jnp.int32, a.shape, 0) + pl.program_id(0) * a.shape[0]
    a_ref[...] = a
    b_ref[...] = jnp.where(row == 0, 0.0, him)
    d_ref[...] = jnp.where(row == 0, nyq_ref[0:1], a)


def _hyena_filters(L, w1, b1, w2, b2, w3, freq):
    t = jnp.linspace(0.0, 1.0, L, dtype=F32)[:, None]
    bands = (HY_POS_EMB - 1) // 2
    fr = jnp.linspace(1e-4, bands - 1, bands, dtype=F32)[None, :]
    w = 2.0 * math.pi * jnp.arange(L, dtype=F32)[:, None] / L
    z = jnp.concatenate([t, jnp.cos(fr * w), -jnp.sin(fr * w)], axis=-1)
    z = jnp.pad(z, ((0, 0), (0, HY_HIDDEN - HY_POS_EMB)))
    w1p = jnp.pad(w1, ((0, 0), (0, HY_HIDDEN - HY_POS_EMB), (0, 0)))
    max_decay = math.log(HY_TARGET) / HY_SHORT_DECAY_PCT
    min_decay = math.log(HY_TARGET) / HY_LONG_DECAY_PCT
    deltas = jnp.linspace(min_decay, max_decay, HY_W, dtype=F32)[None, :]
    tc = LANES
    nct = HY_W // tc
    nw = HY_ORDER * HY_W
    fsum, fdiff, nyq = pl.pallas_call(
        functools.partial(_filt_mlp_kernel, L),
        out_shape=(jax.ShapeDtypeStruct((DEPTH, L, nw), F32),
                   jax.ShapeDtypeStruct((DEPTH, L, nw), F32),
                   jax.ShapeDtypeStruct((DEPTH, 8, nw), F32)),
        grid=(DEPTH, HY_ORDER, nct),
        in_specs=[
            pl.BlockSpec((L, HY_HIDDEN), lambda l, o, j: (0, 0)),
            pl.BlockSpec((None, HY_HIDDEN, HY_HIDDEN), lambda l, o, j: (l, 0, 0)),
            pl.BlockSpec((None, 1, HY_HIDDEN), lambda l, o, j: (l, 0, 0)),
            pl.BlockSpec((None, HY_HIDDEN, HY_HIDDEN), lambda l, o, j: (l, 0, 0)),
            pl.BlockSpec((None, 1, HY_HIDDEN), lambda l, o, j: (l, 0, 0)),
            pl.BlockSpec((None, HY_HIDDEN, tc), lambda l, o, j: (l, 0, o * nct + j)),
            pl.BlockSpec((None, HY_HIDDEN, tc), lambda l, o, j: (l, 0, (HY_ORDER + o) * nct + j)),
            pl.BlockSpec((None, 2, HY_HIDDEN), lambda l, o, j: (l, 0, 0)),
            pl.BlockSpec((1, tc), lambda l, o, j: (0, j)),
        ],
        out_specs=(pl.BlockSpec((None, L, tc), lambda l, o, j: (l, 0, o * nct + j)),
                   pl.BlockSpec((None, L, tc), lambda l, o, j: (l, 0, o * nct + j)),
                   pl.BlockSpec((None, 8, tc), lambda l, o, j: (l, 0, o * nct + j))),
        compiler_params=_cparams(3),
        name="hyena_filter_mlp",
    )(z, w1p, b1.reshape(DEPTH, 1, HY_HIDDEN), w2, b2.reshape(DEPTH, 1, HY_HIDDEN), w3, w3, freq, deltas)

    cm, sm = _dft_mats(L)
    tf = min(L, 512)
    spec = jax.ShapeDtypeStruct((DEPTH, L, nw), F32)
    return pl.pallas_call(
        _filt_spec_kernel,
        out_shape=(spec, spec, spec),
        grid=(L // tf, DEPTH),
        in_specs=[
            pl.BlockSpec((tf, L), lambda i, l: (i, 0)),
            pl.BlockSpec((tf, L), lambda i, l: (i, 0)),
            pl.BlockSpec((None, L, nw), lambda i, l: (l, 0, 0)),
            pl.BlockSpec((None, L, nw), lambda i, l: (l, 0, 0)),
            pl.BlockSpec((None, 8, nw), lambda i, l: (l, 0, 0)),
        ],
        out_specs=(pl.BlockSpec((None, tf, nw), lambda i, l: (l, i, 0)),) * 3,
        compiler_params=_cparams(2),
        name="hyena_filter_spectrum",
    )(cm, sm, fsum, fdiff, nyq)


def _short_conv_kernel(L, z_ref, w_ref, b_ref, o_ref):
    z = z_ref[...]
    w = w_ref[...]
    row = lax.broadcasted_iota(jnp.int32, z.shape, 0)
    prev = jnp.where(row == 0, 0.0, pltpu.roll(z, 1, 0))
    nxt = jnp.where(row == L - 1, 0.0, pltpu.roll(z, L - 1, 0))
    o_ref[...] = b_ref[...] + prev * w[0:1] + z * w[1:2] + nxt * w[2:3]


def _short_conv(z, conv_w, conv_b):
    B, L, _ = z.shape
    W = HY_W
    nb = (HY_ORDER + 1)
    cb = COL_HY // W
    return pl.pallas_call(
        functools.partial(_short_conv_kernel, L),
        out_shape=jax.ShapeDtypeStruct((B, L, nb * W), F32),
        grid=(B, nb),
        in_specs=[pl.BlockSpec((None, L, W), lambda b, j: (b, 0, cb + j)),
                  pl.BlockSpec((3, W), lambda b, j: (0, j)),
                  pl.BlockSpec((1, W), lambda b, j: (0, j))],
        out_specs=pl.BlockSpec((None, L, W), lambda b, j: (b, 0, j)),
        compiler_params=_cparams(2),
        name="hyena_short_conv",
    )(z, conv_w, conv_b.reshape(1, nb * W))


def _conv_fwd_kernel(u_ref, c_ref, s_ref, a_ref, b_ref, d_ref, yre_ref, yim_ref):
    u = u_ref[...]
    ure = _dot(c_ref[...], u)
    uim = -_dot(s_ref[...], u)
    rowu = lax.broadcasted_iota(jnp.int32, u.shape, 0)
    nyq = jnp.sum(jnp.where((rowu & 1) == 0, u, -u), axis=0, keepdims=True)
    row = lax.broadcasted_iota(jnp.int32, ure.shape, 0) + pl.program_id(0) * ure.shape[0]
    uim = jnp.where(row == 0, nyq, uim)
    a = a_ref[...]
    bm = b_ref[...]
    yre_ref[...] = (ure * a - uim * bm).astype(BF16)
    yim_ref[...] = (ure * bm + uim * d_ref[...]).astype(BF16)


def _conv_inv_kernel(L, yre_ref, yim_ref, c_ref, s_ref, u_ref, skip_ref, m_ref, o_ref):
    yre = yre_ref[...]
    yim = yim_ref[...]
    y = (jnp.dot(c_ref[...], yre, preferred_element_type=F32)
         - jnp.dot(s_ref[...], yim, preferred_element_type=F32)) * (1.0 / L)
    row = lax.broadcasted_iota(jnp.int32, y.shape, 0) + pl.program_id(0) * y.shape[0]
    sgn = jnp.where((row & 1) == 0, 1.0, -1.0)
    y = y + (sgn * yim[0:1].astype(F32) - yre[0:1].astype(F32)) * (0.5 / L)
    o_ref[...] = m_ref[...] * (y + u_ref[...] * skip_ref[...])


def _long_conv(zc, u_col, m_col, filt, layer, order, skip, u_arr=None):
    B, L, _ = zc.shape
    W = HY_W
    cm, sm = _dft_mats(L)
    fa, fb, fd = filt
    tf = min(L, 512)
    if u_arr is None:
        u_arr, ub = zc, u_col
    else:
        ub = 0
    spec_t = pl.BlockSpec((tf, L), lambda i, b: (i, 0))
    fspec = pl.BlockSpec((None, tf, W), lambda i, b: (layer, i, order))
    yre, yim = pl.pallas_call(
        _conv_fwd_kernel,
        out_shape=(jax.ShapeDtypeStruct((B, L, W), BF16),) * 2,
        grid=(L // tf, B),
        in_specs=[pl.BlockSpec((None, L, W), lambda i, b: (b, 0, ub)),
                  spec_t, spec_t, fspec, fspec, fspec],
        out_specs=(pl.BlockSpec((None, tf, W), lambda i, b: (b, i, 0)),) * 2,
        compiler_params=_cparams(2),
        name="hyena_conv_fwd",
    )(u_arr, cm, sm, fa, fb, fd)
    full = pl.BlockSpec((None, L, W), lambda i, b: (b, 0, 0))
    return pl.pallas_call(
        functools.partial(_conv_inv_kernel, L),
        out_shape=jax.ShapeDtypeStruct((B, L, W), F32),
        grid=(L // tf, B),
        in_specs=[full, full, spec_t, spec_t,
                  pl.BlockSpec((None, tf, W), lambda i, b: (b, i, ub)),
                  pl.BlockSpec((None, 1, W), lambda i, b: (order, 0, 0)),
                  pl.BlockSpec((None, tf, W), lambda i, b: (b, i, m_col))],
        out_specs=pl.BlockSpec((None, tf, W), lambda i, b: (b, i, 0)),
        compiler_params=_cparams(2),
        name="hyena_conv_inv",
    )(yre, yim, cm, sm, u_arr, skip.reshape(HY_ORDER, 1, W), zc)


def _hyena(z, filt, layer, conv_w, conv_b, skip):
    zc = _short_conv(z, conv_w, conv_b)
    y1 = _long_conv(zc, 0, 1, filt, layer, 0, skip)
    return _long_conv(zc, 0, 2, filt, layer, 1, skip, u_arr=y1)


def _out_kernel(cond_base, cond_stride, hg_ref, da_ref, hy_ref, x_ref, mod_ref, g_ref, w_ref, wr_ref,
                x1_ref, h2_ref, aff_ref):
    row = cond_base + cond_stride * pl.program_id(0)
    D = D_MODEL
    gate1 = mod_ref[pl.ds(row, 1), pl.ds(2 * D, D)]
    shift2 = mod_ref[pl.ds(row, 1), pl.ds(3 * D, D)]
    scale2 = mod_ref[pl.ds(row, 1), pl.ds(4 * D, D)]
    mix = (_dot(hg_ref[...], w_ref[0:HG_W, :])
           + _dot(da_ref[...], w_ref[HG_W:HG_W + DA_W, :])
           + _dot(hy_ref[...], w_ref[HG_W + DA_W:D, :]))
    x1 = x_ref[...] + gate1 * mix
    x1_ref[...] = x1
    h2 = _rms(x1) * g_ref[...] * (1.0 + scale2) + shift2
    h2_ref[...] = h2.astype(BF16)
    logits = _dot_hi(h2, wr_ref[...])
    lane = lax.broadcasted_iota(jnp.int32, logits.shape, 1)
    logits = jnp.where(lane < N_EXPERTS, logits, -jnp.inf)
    e = jnp.exp(logits - jnp.max(logits, axis=1, keepdims=True))
    aff_ref[...] = e / jnp.sum(e, axis=1, keepdims=True)


def _out_proj(o_hg, o_da, o_hy, x, mod_l, g, w_bf, wr_pad, cond_base, cond_stride):
    B, L, D = x.shape
    tm = 256

    def rows(w):
        return pl.BlockSpec((None, tm, w), lambda b, i: (b, i, 0))

    return pl.pallas_call(
        functools.partial(_out_kernel, cond_base, cond_stride),
        out_shape=(jax.ShapeDtypeStruct((B, L, D), F32),
                   jax.ShapeDtypeStruct((B, L, D), BF16),
                   jax.ShapeDtypeStruct((B, L, LANES), F32)),
        grid=(B, L // tm),
        in_specs=[rows(HG_W), rows(DA_W), rows(HY_W), rows(D),
                  pl.BlockSpec((8, N_MOD * D), lambda b, i: (0, 0)),
                  pl.BlockSpec((1, D), lambda b, i: (0, 0)),
                  pl.BlockSpec((D, D), lambda b, i: (0, 0)),
                  pl.BlockSpec((D, LANES), lambda b, i: (0, 0))],
        out_specs=(rows(D), rows(D), rows(LANES)),
        compiler_params=_cparams(2),
        name="out_proj",
    )(o_hg, o_da, o_hy, x, mod_l, g, w_bf, wr_pad)


def _prefix_count(m, blk):
    L = m.shape[0]
    r = lax.broadcasted_iota(jnp.int32, (blk, blk), 0)
    c = lax.broadcasted_iota(jnp.int32, (blk, blk), 1)
    tri = jnp.where(c < r, 1.0, 0.0).astype(BF16)
    carry = jnp.zeros((1, m.shape[1]), F32)
    parts = []
    for i in range(L // blk):
        mb = m[i * blk:(i + 1) * blk]
        parts.append(jnp.dot(tri, mb.astype(BF16), preferred_element_type=F32) + carry)
        carry = carry + jnp.sum(mb, axis=0, keepdims=True)
    return jnp.concatenate(parts, axis=0) if len(parts) > 1 else parts[0]


def _route_kernel(cap, aff_ref, slot_ref, slot_t_ref):
    aff = aff_ref[...]
    capf = float(cap)
    tiny = 2.0 ** -126

    def enough(t):
        return jnp.sum(jnp.where(aff >= t, 1.0, 0.0), axis=0, keepdims=True) >= capf

    has = enough(jnp.full((1, LANES), tiny, F32))
    lo = jnp.full((1, LANES), tiny, F32)
    for s in (64, 32, 16, 8, 4, 2, 1):
        cand = lo * (2.0 ** s)
        lo = jnp.where(enough(cand), cand, lo)
    step = lo
    for _ in range(23):
        step = step * 0.5
        cand = lo + step
        lo = jnp.where(enough(cand), cand, lo)
    hi = jnp.where(has, lo + jnp.maximum(step, tiny), tiny)
    lo = jnp.where(has, lo, 0.0)
    gt = jnp.where(aff >= hi, 1.0, 0.0)
    eq = jnp.where((aff >= lo) & (aff < hi), 1.0, 0.0)
    need = capf - jnp.sum(gt, axis=0, keepdims=True)
    blk = min(aff.shape[0], 256)
    sel = gt + eq * jnp.where(_prefix_count(eq, blk) < need, 1.0, 0.0)
    lane = lax.broadcasted_iota(jnp.int32, aff.shape, 1)
    sel = jnp.where(lane < N_EXPERTS, sel, 0.0)
    slot = jnp.where(sel > 0.0, _prefix_count(sel, blk), -1.0)
    slot_ref[...] = slot
    slot_t_ref[...] = jnp.transpose(slot)[0:N_EXPERTS, :]


def _route(aff, cap):
    B, L, _ = aff.shape
    return pl.pallas_call(
        functools.partial(_route_kernel, cap),
        out_shape=(jax.ShapeDtypeStruct((B, L, LANES), F32),
                   jax.ShapeDtypeStruct((B, N_EXPERTS, L), F32)),
        grid=(B,),
        in_specs=[pl.BlockSpec((None, L, LANES), lambda b: (b, 0, 0))],
        out_specs=(pl.BlockSpec((None, L, LANES), lambda b: (b, 0, 0)),
                   pl.BlockSpec((None, N_EXPERTS, L), lambda b: (b, 0, 0))),
        compiler_params=_cparams(1),
        name="route",
    )(aff)


def _gather_kernel(cap, slot_t_ref, h_ref, xs_ref):
    e = pl.program_id(1)
    slot = slot_t_ref[pl.ds(e, 1), :]
    c = lax.broadcasted_iota(jnp.int32, (cap, slot.shape[1]), 0).astype(F32)
    onehot = jnp.where(c == slot, 1.0, 0.0).astype(BF16)
    xs_ref[...] = jnp.dot(onehot, h_ref[...], preferred_element_type=F32).astype(BF16)


def _gather(slot_t, h2, cap):
    B, L, D = h2.shape
    return pl.pallas_call(
        functools.partial(_gather_kernel, cap),
        out_shape=jax.ShapeDtypeStruct((N_EXPERTS, B * cap, D), BF16),
        grid=(B, N_EXPERTS),
        in_specs=[pl.BlockSpec((None, N_EXPERTS, L), lambda b, e: (b, 0, 0)),
                  pl.BlockSpec((None, L, D), lambda b, e: (b, 0, 0))],
        out_specs=pl.BlockSpec((None, cap, D), lambda b, e: (e, b, 0)),
        compiler_params=_cparams(2),
        name="expert_gather",
    )(slot_t, h2)


def _ffn_kernel(n_f, xa_ref, xb_ref, wg_ref, wu_ref, wd_ref, oa_ref, ob_ref, acc_ref):
    f = pl.program_id(1)
    wg = wg_ref[...].astype(BF16)
    wu = wu_ref[...].astype(BF16)
    wd = wd_ref[...].astype(BF16)
    for i, x_ref in enumerate((xa_ref, xb_ref)):
        x = x_ref[...]
        hid = (_silu(jnp.dot(x, wg, preferred_element_type=F32))
               * jnp.dot(x, wu, preferred_element_type=F32))
        part = jnp.dot(hid.astype(BF16), wd, preferred_element_type=F32)

        @pl.when(f == 0)
        def _():
            acc_ref[i] = part

        @pl.when(f > 0)
        def _():
            acc_ref[i] = acc_ref[i] + part

    @pl.when(f == n_f - 1)
    def _():
        oa_ref[...] = acc_ref[0].astype(BF16)
        ob_ref[...] = acc_ref[1].astype(BF16)


def _expert_ffn(xs_a, xs_b, w_gate, w_up, w_down):
    E, S, D = xs_a.shape
    tf = 512
    n_f = D_FF // tf
    xspec = pl.BlockSpec((None, S, D), lambda e, f: (e, 0, 0))
    out = jax.ShapeDtypeStruct((E, S, D), BF16)
    return pl.pallas_call(
        functools.partial(_ffn_kernel, n_f),
        out_shape=(out, out),
        grid=(E, n_f),
        in_specs=[xspec, xspec,
                  pl.BlockSpec((None, D, tf), lambda e, f: (e, 0, f)),
                  pl.BlockSpec((None, D, tf), lambda e, f: (e, 0, f)),
                  pl.BlockSpec((None, tf, D), lambda e, f: (e, f, 0))],
        out_specs=(xspec, xspec),
        scratch_shapes=[pltpu.VMEM((2, S, D), F32)],
        compiler_params=_cparams(2),
        name="expert_ffn",
    )(xs_a, xs_b, w_gate, w_up, w_down)


def _scatter_kernel(cap, cond_base, cond_stride, final, ys_ref, slot_ref, aff_ref, x_ref, mod_ref, fg_ref,
                    o_ref):
    row = cond_base + cond_stride * pl.program_id(0)
    gate2 = mod_ref[pl.ds(row, 1), pl.ds(5 * D_MODEL, D_MODEL)]
    slot = slot_ref[...]
    aff = aff_ref[...]
    tt = slot.shape[0]
    c = lax.broadcasted_iota(jnp.int32, (tt, cap), 1).astype(F32)
    y = jnp.zeros((tt, D_MODEL), F32)
    for e in range(N_EXPERTS):
        onehot = jnp.where(c == slot[:, e:e + 1], 1.0, 0.0).astype(BF16)
        y = y + aff[:, e:e + 1] * jnp.dot(onehot, ys_ref[e], preferred_element_type=F32)
    x2 = x_ref[...] + gate2 * y
    if final:
        x2 = _rms(x2) * fg_ref[...]
    o_ref[...] = x2


def _scatter(ys, slot, aff, x1, mod_l, fg, cap, cond_base, cond_stride, final):
    B, L, D = x1.shape
    tt = min(L, 512)

    def rows(w):
        return pl.BlockSpec((None, tt, w), lambda b, i: (b, i, 0))

    return pl.pallas_call(
        functools.partial(_scatter_kernel, cap, cond_base, cond_stride, final),
        out_shape=jax.ShapeDtypeStruct((B, L, D), F32),
        grid=(B, L // tt),
        in_specs=[pl.BlockSpec((N_EXPERTS, cap, D), lambda b, i: (0, b, 0)),
                  rows(LANES), rows(LANES), rows(D),
                  pl.BlockSpec((8, N_MOD * D), lambda b, i: (0, 0)),
                  pl.BlockSpec((1, D), lambda b, i: (0, 0))],
        out_specs=rows(D),
        compiler_params=_cparams(2),
        name="expert_scatter",
    )(ys, slot, aff, x1, mod_l, fg)


def _state_to_kernel_layout(s):
    B = s.shape[0]
    st = jnp.swapaxes(s, -1, -2)
    eye = jnp.eye(HG_HEADS, dtype=s.dtype)
    full = st[:, :, :, :, None, :] * eye[None, None, :, None, :, None]
    return full.reshape(B, 2, HG_W, HG_W)


def _state_from_kernel_layout(st):
    B = st.shape[0]
    full = st.reshape(B, 2, HG_HEADS, HG_DK, HG_HEADS, HG_DK)
    diag = jnp.stack([full[:, :, h, :, h, :] for h in range(HG_HEADS)], axis=2)
    return jnp.swapaxes(diag, -1, -2)


def kernel(x_prompt, x_sample, c, cache_diff_k, cache_diff_v, state_hgrn, c_ctx, norm_g, final_norm_g,
           w_mod, b_mod, w_in, w_out, hgrn_lb_logits, hgrn_norm_g, diff_lambda, diff_subln_g,
           hy_conv_w, hy_conv_b, hy_filt_w1, hy_filt_b1, hy_filt_w2, hy_filt_b2, hy_filt_w3, hy_filt_freq,
           hy_skip, w_router, w_gate, w_up, w_down):
    B_p, L_p, D = x_prompt.shape
    B_s, L_s, _ = x_sample.shape
    cap_p = EC_CAPACITY_FACTOR * L_p // N_EXPERTS
    cap_s = EC_CAPACITY_FACTOR * L_s // N_EXPERTS
    assert B_p * cap_p == B_s * cap_s

    cond8 = jnp.zeros((8, D), F32).at[0].set(c_ctx.astype(F32)).at[1:1 + B_s].set(c.astype(F32))
    mod = _modulation(cond8, w_mod, b_mod)
    groups = (
        dict(cond=(0, 0), cap=cap_p, L=L_p),
        dict(cond=(1, 1), cap=cap_s, L=L_s),
    )
    filt = {L: _hyena_filters(L, hy_filt_w1, hy_filt_b1, hy_filt_w2, hy_filt_b2, hy_filt_w3, hy_filt_freq)
            for L in {L_p, L_s}}
    rope_tabs = _rope_tables(L_s)
    fg = final_norm_g.astype(F32).reshape(1, D)
    wr_pad = jnp.pad(w_router.astype(F32), ((0, 0), (0, 0), (0, LANES - N_EXPERTS)))
    s0 = (jnp.zeros((B_p, 2, HG_W, HG_W), F32), None)

    xs = [x_prompt.astype(F32), x_sample.astype(F32)]
    new_k, new_v, new_s = [], [], []
    for l in range(DEPTH):
        w_in_bf = w_in[l].astype(BF16)
        w_out_bf = w_out[l].astype(BF16)
        ng = jnp.tile(hgrn_norm_g[l].astype(F32), HG_HEADS).reshape(1, HG_W)
        sg = diff_subln_g[l].astype(F32).reshape(1, DA_DV)
        g1 = norm_g[l, 0].astype(F32).reshape(1, D)
        g2 = norm_g[l, 1].astype(F32).reshape(1, D)
        x1s, slots, affs, gathered = [], [], [], []
        for gi, grp in enumerate(groups):
            x = xs[gi]
            cb, cs = grp["cond"]
            z = _in_proj(x, mod[l], g1, w_in_bf, cb, cs)
            if gi == 0:
                s_init = s0[0]
            else:
                s_init = _state_to_kernel_layout(state_hgrn[:, l].astype(F32))
            o_hg, s_fin = _hgrn(z, hgrn_lb_logits.astype(F32), s_init, ng, l)
            if gi == 0:
                B, L = B_p, L_p
                new_k.append(z[:, :, COL_AK:COL_AV].reshape(B, L, DA_HEADS, 2, DA_DQK).transpose(0, 2, 3, 1, 4))
                new_v.append(z[:, :, COL_AV:COL_HY].reshape(B, L, DA_HEADS, DA_DV).transpose(0, 2, 1, 3))
                new_s.append(_state_from_kernel_layout(s_fin))
                o_da = _attention(z, l, diff_lambda[l].astype(F32), sg)
            else:
                o_da = _attention(z, l, diff_lambda[l].astype(F32), sg,
                                  cache_diff_k.astype(F32), cache_diff_v.astype(F32), rope_tabs)
            o_hy = _hyena(z, filt[grp["L"]], l, hy_conv_w[l].astype(F32), hy_conv_b[l].astype(F32),
                          hy_skip[l].astype(F32))
            x1, h2, aff = _out_proj(o_hg, o_da, o_hy, x, mod[l], g2, w_out_bf, wr_pad[l], cb, cs)
            slot, slot_t = _route(aff, grp["cap"])
            x1s.append(x1)
            slots.append(slot)
            affs.append(aff)
            gathered.append(_gather(slot_t, h2, grp["cap"]))
        ys = _expert_ffn(gathered[0], gathered[1], w_gate[l], w_up[l], w_down[l])
        for gi, grp in enumerate(groups):
            cb, cs = grp["cond"]
            xs[gi] = _scatter(ys[gi], slots[gi], affs[gi], x1s[gi], mod[l], fg, grp["cap"], cb, cs,
                              final=(l == DEPTH - 1))

    dt = x_prompt.dtype
    return (xs[0].astype(dt), xs[1].astype(x_sample.dtype),
            jnp.stack(new_k, axis=1).astype(dt), jnp.stack(new_v, axis=1).astype(dt),
            jnp.stack(new_s, axis=1).astype(dt))
```

```python
import functools
import math

import jax
import jax.numpy as jnp
import numpy as np
from jax import lax
from jax.experimental import pallas as pl
from jax.experimental.pallas import tpu as pltpu

F32 = jnp.float32
BF16 = jnp.bfloat16
HI = lax.Precision.HIGHEST

D_MODEL = 1024
DEPTH = 4
GRID_W = 64
HG_HEADS = 4
HG_DK = 64
HG_W = 256
DA_HEADS = 4
DA_DQK = 64
DA_DV = 128
DA_W = 512
HY_W = 256
HY_ORDER = 2
HY_POS_EMB = 33
HY_HIDDEN = 64
HY_SHORT_DECAY_PCT = 0.3
HY_LONG_DECAY_PCT = 1.5
HY_TARGET = 1e-2
N_EXPERTS = 16
EC_CAPACITY_FACTOR = 2
D_FF = 1024
N_MOD = 6
ROPE_BASE = 10000.0
NORM_EPS = 1e-6
D_IN = 3584
COL_HQ, COL_HFF, COL_HFB, COL_HI, COL_HG = 0, 256, 512, 768, 1024
COL_AQ, COL_AK, COL_AV, COL_HY = 1280, 1792, 2304, 2816
Z_W = COL_AQ + (D_IN - COL_HY)
Z_HY = COL_AQ

LANES = 128
HGRN_CHUNK = 16
HGRN_TILE = 8
HGRN_GROUP = 4
GATHER_ROWS = 1024
VMEM_LIMIT = 56 * 1024 * 1024


def _cparams(n_axes):
    return pltpu.CompilerParams(dimension_semantics=("arbitrary",) * n_axes,
                                vmem_limit_bytes=VMEM_LIMIT)


def _dot(a, b):
    return jnp.dot(a.astype(BF16), b.astype(BF16), preferred_element_type=F32)


def _dot_nt(a, b):
    return lax.dot_general(a.astype(BF16), b.astype(BF16), (((1,), (1,)), ((), ())),
                           preferred_element_type=F32)


def _dot_tn(a, b):
    return lax.dot_general(a.astype(BF16), b.astype(BF16), (((0,), (0,)), ((), ())),
                           preferred_element_type=F32)


def _dot_hi(a, b):
    return jnp.dot(a, b, precision=HI, preferred_element_type=F32)


def _sigmoid(x):
    return 1.0 / (1.0 + jnp.exp(-x))


def _silu(x):
    return x * _sigmoid(x)


def _mod_kernel(cond_ref, w_ref, b_ref, o_ref):
    s = _silu(cond_ref[...])
    o_ref[...] = _dot(s, w_ref[...]) + b_ref[...]


def _modulation(cond8, w_mod, b_mod):
    tn = 1536
    n_mod = N_MOD * D_MODEL
    return pl.pallas_call(
        _mod_kernel,
        out_shape=jax.ShapeDtypeStruct((DEPTH, 8, n_mod), F32),
        grid=(DEPTH, n_mod // tn),
        in_specs=[
            pl.BlockSpec((8, D_MODEL), lambda l, j: (0, 0)),
            pl.BlockSpec((None, D_MODEL, tn), lambda l, j: (l, 0, j)),
            pl.BlockSpec((None, 1, tn), lambda l, j: (l, 0, j)),
        ],
        out_specs=pl.BlockSpec((None, 8, tn), lambda l, j: (l, 0, j)),
        compiler_params=_cparams(2),
        name="modulation",
    )(cond8, w_mod, b_mod.reshape(DEPTH, 1, n_mod))


def _rms(x):
    return x * lax.rsqrt(jnp.mean(x * x, axis=-1, keepdims=True) + NORM_EPS)


def _rope(x, cos, sin):
    lane = lax.broadcasted_iota(jnp.int32, x.shape, 1)
    swapped = jnp.where((lane & 31) < 16, -pltpu.roll(x, LANES - 16, 1), pltpu.roll(x, 16, 1))
    return x * cos + swapped * sin


def _in_kernel(cond_base, cond_stride, latent, x_ref, mod_ref, g_ref, w_ref, *rest):
    if latent:
        cos_ref, sin_ref, z_ref, q_ref, k_ref, v_ref = rest
    else:
        z_ref, q_ref, k_ref, v_ref, kc_ref, vc_ref = rest[-6:]
    row = cond_base + cond_stride * pl.program_id(0)
    shift = mod_ref[pl.ds(row, 1), pl.ds(0, D_MODEL)]
    scale = mod_ref[pl.ds(row, 1), pl.ds(D_MODEL, D_MODEL)]
    h = (_rms(x_ref[...]) * g_ref[...] * (1.0 + scale) + shift).astype(BF16)
    z_ref[:, 0:COL_AQ] = jnp.dot(h, w_ref[:, 0:COL_AQ], preferred_element_type=F32)
    z_ref[:, COL_AQ:Z_W] = jnp.dot(h, w_ref[:, COL_HY:D_IN], preferred_element_type=F32)
    att = jnp.dot(h, w_ref[:, COL_AQ:COL_HY], preferred_element_type=F32)
    q = att[:, 0:DA_W]
    k = att[:, DA_W:2 * DA_W]
    v = att[:, 2 * DA_W:3 * DA_W]
    if latent:
        cos = cos_ref[...]
        sin = sin_ref[...]
        heads = [slice(hh * LANES, (hh + 1) * LANES) for hh in range(DA_HEADS)]
        q = jnp.concatenate([_rope(q[:, s], cos, sin) for s in heads], axis=1)
        k = jnp.concatenate([_rope(k[:, s], cos, sin) for s in heads], axis=1)
    else:
        for hh in range(DA_HEADS):
            for m in range(2):
                c0 = (2 * hh + m) * DA_DQK
                kc_ref[hh, m] = k[:, c0:c0 + DA_DQK].astype(kc_ref.dtype)
            vc_ref[hh] = v[:, hh * DA_DV:(hh + 1) * DA_DV].astype(vc_ref.dtype)
    q_ref[...] = (q * (DA_DQK ** -0.5)).astype(BF16)
    k_ref[...] = k.astype(BF16)
    v_ref[...] = v.astype(BF16)


def _in_proj(x, mod_l, g, w_bf, cond_base, cond_stride, layer, rope_tabs=None, caches=None, cache_dtype=None):
    B, L, _ = x.shape
    tm = 256
    latent = rope_tabs is not None
    rows = lambda w: pl.BlockSpec((None, tm, w), lambda b, i: (b, i, 0))
    in_specs = [
        rows(D_MODEL),
        pl.BlockSpec((8, N_MOD * D_MODEL), lambda b, i: (0, 0)),
        pl.BlockSpec((1, D_MODEL), lambda b, i: (0, 0)),
        pl.BlockSpec((D_MODEL, D_IN), lambda b, i: (0, 0)),
    ]
    args = [x, mod_l, g, w_bf]
    out_shape = [jax.ShapeDtypeStruct((B, L, Z_W), F32)] + [jax.ShapeDtypeStruct((B, L, DA_W), BF16)] * 3
    out_specs = [rows(Z_W), rows(DA_W), rows(DA_W), rows(DA_W)]
    aliases = {}
    if latent:
        in_specs += [pl.BlockSpec((tm, LANES), lambda b, i: (i, 0))] * 2
        args += list(rope_tabs)
    else:
        assert tm == L
        out_shape += [jax.ShapeDtypeStruct((B, DEPTH, DA_HEADS, 2, L, DA_DQK), cache_dtype),
                      jax.ShapeDtypeStruct((B, DEPTH, DA_HEADS, L, DA_DV), cache_dtype)]
        out_specs += [pl.BlockSpec((None, None, DA_HEADS, 2, tm, DA_DQK), lambda b, i: (b, layer, 0, 0, i, 0)),
                      pl.BlockSpec((None, None, DA_HEADS, tm, DA_DV), lambda b, i: (b, layer, 0, i, 0))]
        if caches is not None:
            in_specs += [pl.BlockSpec(memory_space=pl.ANY)] * 2
            args += list(caches)
            aliases = {4: 4, 5: 5}
    return pl.pallas_call(
        functools.partial(_in_kernel, cond_base, cond_stride, latent),
        out_shape=tuple(out_shape),
        grid=(B, L // tm),
        in_specs=in_specs,
        out_specs=tuple(out_specs),
        input_output_aliases=aliases,
        compiler_params=_cparams(2),
        name="in_proj",
    )(*args)


def _head_block_mask(n, width):
    r = lax.broadcasted_iota(jnp.int32, (n, n), 0) // width
    c = lax.broadcasted_iota(jnp.int32, (n, n), 1) // width
    return r == c


def _hgrn_kernel(L, layer, q_ref, ff_ref, fb_ref, v_ref, g_ref, lbl_ref, s0_ref, ng_ref,
                 o_ref, sfin_ref, bf_s, bb_s, kf_s, kb_s, of_s, ob_s, st_s):
    C = HGRN_CHUNK
    T = HGRN_TILE
    R = HGRN_GROUP * C
    W = HG_W
    HALF = W // 2
    ng_groups = L // R

    lg = [lbl_ref[j] for j in range(DEPTH)]
    mx = functools.reduce(jnp.maximum, lg)
    ex = [jnp.exp(a - mx) for a in lg]
    den = functools.reduce(lambda a, b: a + b, ex)
    sm = [e / den for e in ex]
    cum = sm[0]
    for j in range(1, layer + 1):
        cum = cum + sm[j]
    lb = cum - sm[0]

    rowc = lax.broadcasted_iota(jnp.int32, (L, W), 0) & (C - 1)

    def gates(f, lbd):
        e = jnp.exp(-jnp.abs(f))
        r = 1.0 / (1.0 + e)
        log_sig = jnp.minimum(f, 0.0) + jnp.log(r)
        sig_neg = jnp.where(f >= 0.0, e * r, r)
        if layer == 0:
            return sig_neg, log_sig
        a = jnp.log(lbd)
        c = jnp.log1p(-lbd) + log_sig
        logf = jnp.maximum(a, c) + jnp.log1p(jnp.exp(-jnp.abs(a - c)))
        return (1.0 - lbd) * sig_neg, logf

    kf, lf = gates(ff_ref[...], lb[0:1])
    kb, lbk = gates(fb_ref[...], lb[1:2])
    kf_s[...] = kf
    kb_s[...] = kb
    d = 1
    while d < C:
        lf = lf + jnp.where(rowc >= d, pltpu.roll(lf, d, 0), 0.0)
        lbk = lbk + jnp.where(rowc < C - d, pltpu.roll(lbk, L - d, 0), 0.0)
        d *= 2
    bf_s[...] = lf
    bb_s[...] = lbk

    st_s[...] = s0_ref[...]
    bd_bf = jnp.where(_head_block_mask(W, HG_DK), 1.0, 0.0).astype(BF16)
    bd_half = _head_block_mask(HALF, HG_DK)
    n_tiles = R // T
    trow = lax.broadcasted_iota(jnp.int32, (n_tiles, T, W), 1)

    def head_sum(p):
        n = p.shape[0] * p.shape[1]
        return jnp.dot(p.reshape(n, W).astype(BF16), bd_bf,
                       preferred_element_type=F32).reshape(p.shape)

    def bcast(x, s):
        return jnp.broadcast_to(x[:, s:s + 1, :], x.shape[:1] + (T,) + x.shape[2:])

    def group(r0, b_s, k_s, o_s, d, forward):
        sl = pl.ds(r0, R)
        q = q_ref[sl, :]
        v = v_ref[sl, :]
        b = b_s[sl, :]
        k = k_s[sl, :]
        q3, v3, b3, k3 = (x.reshape(n_tiles, T, W) for x in (q, v, b, k))
        acc = jnp.zeros((n_tiles, T, W), F32)
        for s in range(T):
            keep = (trow >= s) if forward else (trow <= s)
            p = jnp.where(keep, jnp.exp(b3 - bcast(b3, s)), 0.0) * q3 * bcast(k3, s)
            acc = acc + head_sum(p) * bcast(v3, s)
        nch = R // C
        q4, v4, b4, k4, acc4 = (x.reshape(nch, C, W) for x in (q, v, b, k, acc.reshape(R, W)))
        if forward:
            tgt, src = slice(T, C), slice(0, T)
            ref = bcast(b4[:, 0:T], T - 1)
        else:
            tgt, src = slice(0, T), slice(T, C)
            ref = bcast(b4[:, T:C], 0)
        qp = q4[:, tgt] * jnp.exp(b4[:, tgt] - ref)
        kp = k4[:, src] * jnp.exp(ref - b4[:, src])
        vs = v4[:, src]
        cross = jnp.zeros((nch, T, W), F32)
        for s in range(T):
            cross = cross + head_sum(qp * bcast(kp, s)) * bcast(vs, s)
        if forward:
            intra = jnp.concatenate([acc4[:, 0:T], acc4[:, T:C] + cross], axis=1)
            b_end = jnp.broadcast_to(b4[:, C - 1:C], b4.shape)
        else:
            intra = jnp.concatenate([acc4[:, 0:T] + cross, acc4[:, T:C]], axis=1)
            b_end = jnp.broadcast_to(b4[:, 0:1], b4.shape)
        qe = (q4 * jnp.exp(b4)).reshape(R, W)
        ke = (k4 * jnp.exp(b_end - b4)).reshape(R, W)
        dec = jnp.exp(b_end).reshape(R, W)
        intra = intra.reshape(R, W)
        st = [st_s[d, p] for p in range(2)]
        outs = [None] * nch
        for c in (range(nch) if forward else range(nch - 1, -1, -1)):
            rows = slice(c * C, (c + 1) * C)
            parts = []
            for p in range(2):
                ln = slice(p * HALF, (p + 1) * HALF)
                parts.append(_dot_nt(qe[rows, ln], st[p]))
                kv = _dot_tn(v[rows, ln], ke[rows, ln])
                st[p] = st[p] * dec[c * C:c * C + 1, ln] + jnp.where(bd_half, kv, 0.0)
            outs[c] = intra[rows] + jnp.concatenate(parts, axis=1)
        for p in range(2):
            st_s[d, p] = st[p]
        o_s[sl, :] = jnp.concatenate(outs, axis=0)

    def body(i, carry):
        group(pl.multiple_of(i * R, R), bf_s, kf_s, of_s, 0, True)
        group(pl.multiple_of((ng_groups - 1 - i) * R, R), bb_s, kb_s, ob_s, 1, False)
        return carry

    lax.fori_loop(0, ng_groups, body, 0)

    o = of_s[...] + ob_s[...]
    ms = _dot_hi(o * o, jnp.where(_head_block_mask(W, HG_DK), 1.0 / HG_DK, 0.0).astype(F32))
    o_ref[...] = o * lax.rsqrt(ms + NORM_EPS) * ng_ref[...] * _silu(g_ref[...])
    sfin_ref[...] = st_s[...]


def _hgrn(z, lb_logits, s0t, ng_tiled, layer):
    B, L, _ = z.shape
    W = HG_W
    HALF = W // 2

    def col(c):
        return pl.BlockSpec((None, L, W), lambda b, c=c: (b, 0, c // W))

    seq = pltpu.VMEM((L, W), F32)
    st_spec = pl.BlockSpec((None, 2, 2, HALF, HALF), lambda b: (b, 0, 0, 0, 0))
    return pl.pallas_call(
        functools.partial(_hgrn_kernel, L, layer),
        out_shape=(jax.ShapeDtypeStruct((B, L, W), F32),
                   jax.ShapeDtypeStruct((B, 2, 2, HALF, HALF), F32)),
        grid=(B,),
        in_specs=[col(COL_HQ), col(COL_HFF), col(COL_HFB), col(COL_HI), col(COL_HG),
                  pl.BlockSpec((DEPTH, 2, W), lambda b: (0, 0, 0)),
                  st_spec,
                  pl.BlockSpec((1, W), lambda b: (0, 0))],
        out_specs=(pl.BlockSpec((None, L, W), lambda b: (b, 0, 0)), st_spec),
        scratch_shapes=[seq, seq, seq, seq, seq, seq, pltpu.VMEM((2, 2, HALF, HALF), F32)],
        compiler_params=_cparams(1),
        name="hgrn",
    )(z, z, z, z, z, lb_logits, s0t, ng_tiled)


def _attn_kernel(latent, lam_init, n_heads, q_ref, k_ref, v_ref, *rest):
    if latent:
        ck_ref, cv_ref, lam_ref, sg_ref, o_ref = rest
    else:
        lam_ref, sg_ref, o_ref = rest
    lv = lam_ref[...]
    lam = (jnp.exp(jnp.sum(lv[0:1] * lv[1:2], axis=1, keepdims=True))
           - jnp.exp(jnp.sum(lv[2:3] * lv[3:4], axis=1, keepdims=True)) + lam_init)
    for hh in range(n_heads):
        hs = slice(hh * LANES, (hh + 1) * LANES)
        q = q_ref[:, hs]
        k = k_ref[:, hs]
        v = v_ref[:, hs]
        lane = lax.broadcasted_iota(jnp.int32, q.shape, 1)
        outs = []
        for m in range(2):
            qm = jnp.where((lane >= m * DA_DQK) & (lane < (m + 1) * DA_DQK), q, jnp.zeros_like(q))
            s = lax.dot_general(qm, k, (((1,), (1,)), ((), ())), preferred_element_type=F32)
            mx = jnp.max(s, axis=1, keepdims=True)
            if latent:
                sc = _dot_nt(q[:, m * DA_DQK:(m + 1) * DA_DQK], ck_ref[hh, m])
                mx = jnp.maximum(mx, jnp.max(sc, axis=1, keepdims=True))
            p = jnp.exp(s - mx)
            den = jnp.sum(p, axis=1, keepdims=True)
            o = jnp.dot(p.astype(BF16), v, preferred_element_type=F32)
            if latent:
                pc = jnp.exp(sc - mx)
                den = den + jnp.sum(pc, axis=1, keepdims=True)
                o = o + _dot(pc, cv_ref[hh])
            outs.append(o / den)
        o = outs[0] - lam * outs[1]
        o_ref[:, hs] = _rms(o) * sg_ref[...] * (1.0 - lam_init)


def _attention(q, k, v, layer, lam_p, subln_g, ctx_k=None, ctx_v=None):
    B, L, _ = q.shape
    latent = ctx_k is not None
    lam_init = 0.8 - 0.6 * math.exp(-0.3 * layer)
    if latent:
        tq, nh = 256, 1
        grid = (B, DA_HEADS, L // tq)
        P = ctx_k.shape[4]
        in_specs = [
            pl.BlockSpec((None, tq, LANES), lambda b, h, i: (b, i, h)),
            pl.BlockSpec((None, L, LANES), lambda b, h, i: (b, 0, h)),
            pl.BlockSpec((None, L, LANES), lambda b, h, i: (b, 0, h)),
            pl.BlockSpec((None, None, 1, 2, P, DA_DQK), lambda b, h, i: (b, layer, h, 0, 0, 0)),
            pl.BlockSpec((None, None, 1, P, DA_DV), lambda b, h, i: (b, layer, h, 0, 0)),
            pl.BlockSpec((4, DA_DQK), lambda b, h, i: (0, 0)),
            pl.BlockSpec((1, DA_DV), lambda b, h, i: (0, 0)),
        ]
        args = [q, k, v, ctx_k, ctx_v, lam_p, subln_g]
        out_spec = pl.BlockSpec((None, tq, DA_DV), lambda b, h, i: (b, i, h))
    else:
        nh = DA_HEADS
        grid = (B,)
        full = pl.BlockSpec((None, L, DA_W), lambda b: (b, 0, 0))
        in_specs = [full, full, full,
                    pl.BlockSpec((4, DA_DQK), lambda b: (0, 0)),
                    pl.BlockSpec((1, DA_DV), lambda b: (0, 0))]
        args = [q, k, v, lam_p, subln_g]
        out_spec = full
    return pl.pallas_call(
        functools.partial(_attn_kernel, latent, lam_init, nh),
        out_shape=jax.ShapeDtypeStruct((B, L, DA_W), F32),
        grid=grid,
        in_specs=in_specs,
        out_specs=out_spec,
        compiler_params=_cparams(len(grid)),
        name="diff_attention",
    )(*args)


def _rope_tables(L):
    half = DA_DQK // 2
    inv = ROPE_BASE ** (-jnp.arange(0, half, 2, dtype=F32) / half)
    t = jnp.arange(L)
    rows = (t // GRID_W).astype(F32)[:, None] * inv[None, :]
    cols = (t % GRID_W).astype(F32)[:, None] * inv[None, :]
    ang = jnp.concatenate([rows, rows, cols, cols], axis=-1)
    ang = jnp.concatenate([ang, ang], axis=-1)
    return jnp.cos(ang), jnp.sin(ang)


@functools.lru_cache(maxsize=None)
def _dft_np(L):
    idx = (np.arange(L, dtype=np.int64)[:, None] * np.arange(L, dtype=np.int64)[None, :]) % (2 * L)
    ang = idx.astype(np.float64) * (math.pi / L)
    return np.cos(ang).astype(np.float32), np.sin(ang).astype(np.float32)


def _dft_mats(L):
    c, s = _dft_np(L)
    return jnp.asarray(c).astype(BF16), jnp.asarray(s).astype(BF16)


def _filt_hidden_kernel(z_ref, w1_ref, b1_ref, w2_ref, b2_ref, fr_ref, hid_ref):
    fr = fr_ref[...]
    hid = jnp.sin(fr[0:1] * (_dot_hi(z_ref[...], w1_ref[...]) + b1_ref[...]))
    hid_ref[...] = jnp.sin(fr[1:2] * (_dot_hi(hid, w2_ref[...]) + b2_ref[...]))


def _filt_taps_kernel(z_ref, hid_ref, w3f_ref, w3b_ref, dl_ref, sum_ref, diff_ref, nyq_ref):
    hid = hid_ref[...]
    decay = jnp.exp(-z_ref[:, 0:1] * jnp.abs(dl_ref[...]))
    hf = _dot_hi(hid, w3f_ref[...]) * decay
    hb = _dot_hi(hid, w3b_ref[...]) * decay
    nrm = (jnp.sum(jnp.abs(hf), axis=0, keepdims=True)
           + jnp.sum(jnp.abs(hb), axis=0, keepdims=True))
    row = lax.broadcasted_iota(jnp.int32, hf.shape, 0)
    hf = hf / nrm
    hb = jnp.where(row == 0, 0.0, hb / nrm)
    fs = hf + hb
    sum_ref[...] = fs
    diff_ref[...] = hf - hb
    sgn = jnp.where((row & 1) == 0, 1.0, -1.0)
    nyq_ref[...] = jnp.broadcast_to(jnp.sum(fs * sgn, axis=0, keepdims=True), nyq_ref.shape)


def _filt_spec_kernel(c_ref, s_ref, sum_ref, diff_ref, nyq_ref, a_ref, b_ref, d_ref):
    a = _dot(c_ref[...], sum_ref[...])
    him = -_dot(s_ref[...], diff_ref[...])
    row = lax.broadcasted_iota(jnp.int32, a.shape, 0) + pl.program_id(0) * a.shape[0]
    a_ref[...] = a
    b_ref[...] = jnp.where(row == 0, 0.0, him)
    d_ref[...] = jnp.where(row == 0, nyq_ref[0:1], a)


def _hyena_filters(L, w1, b1, w2, b2, w3, freq):
    t = jnp.linspace(0.0, 1.0, L, dtype=F32)[:, None]
    bands = (HY_POS_EMB - 1) // 2
    fr = jnp.linspace(1e-4, bands - 1, bands, dtype=F32)[None, :]
    w = 2.0 * math.pi * jnp.arange(L, dtype=F32)[:, None] / L
    z = jnp.concatenate([t, jnp.cos(fr * w), -jnp.sin(fr * w)], axis=-1)
    z = jnp.pad(z, ((0, 0), (0, HY_HIDDEN - HY_POS_EMB)))
    w1p = jnp.pad(w1, ((0, 0), (0, HY_HIDDEN - HY_POS_EMB), (0, 0)))
    max_decay = math.log(HY_TARGET) / HY_SHORT_DECAY_PCT
    min_decay = math.log(HY_TARGET) / HY_LONG_DECAY_PCT
    deltas = jnp.linspace(min_decay, max_decay, HY_W, dtype=F32)[None, :]
    tc = LANES
    nct = HY_W // tc
    nw = HY_ORDER * HY_W
    hid = pl.pallas_call(
        _filt_hidden_kernel,
        out_shape=jax.ShapeDtypeStruct((DEPTH, L, HY_HIDDEN), F32),
        grid=(DEPTH,),
        in_specs=[
            pl.BlockSpec((L, HY_HIDDEN), lambda l: (0, 0)),
            pl.BlockSpec((None, HY_HIDDEN, HY_HIDDEN), lambda l: (l, 0, 0)),
            pl.BlockSpec((None, 1, HY_HIDDEN), lambda l: (l, 0, 0)),
            pl.BlockSpec((None, HY_HIDDEN, HY_HIDDEN), lambda l: (l, 0, 0)),
            pl.BlockSpec((None, 1, HY_HIDDEN), lambda l: (l, 0, 0)),
            pl.BlockSpec((None, 2, HY_HIDDEN), lambda l: (l, 0, 0)),
        ],
        out_specs=pl.BlockSpec((None, L, HY_HIDDEN), lambda l: (l, 0, 0)),
        compiler_params=_cparams(1),
        name="hyena_filter_hidden",
    )(z, w1p, b1.reshape(DEPTH, 1, HY_HIDDEN), w2, b2.reshape(DEPTH, 1, HY_HIDDEN), freq)
    fsum, fdiff, nyq = pl.pallas_call(
        _filt_taps_kernel,
        out_shape=(jax.ShapeDtypeStruct((DEPTH, L, nw), F32),
                   jax.ShapeDtypeStruct((DEPTH, L, nw), F32),
                   jax.ShapeDtypeStruct((DEPTH, 8, nw), F32)),
        grid=(DEPTH, HY_ORDER, nct),
        in_specs=[
            pl.BlockSpec((L, HY_HIDDEN), lambda l, o, j: (0, 0)),
            pl.BlockSpec((None, L, HY_HIDDEN), lambda l, o, j: (l, 0, 0)),
            pl.BlockSpec((None, HY_HIDDEN, tc), lambda l, o, j: (l, 0, o * nct + j)),
            pl.BlockSpec((None, HY_HIDDEN, tc), lambda l, o, j: (l, 0, (HY_ORDER + o) * nct + j)),
            pl.BlockSpec((1, tc), lambda l, o, j: (0, j)),
        ],
        out_specs=(pl.BlockSpec((None, L, tc), lambda l, o, j: (l, 0, o * nct + j)),
                   pl.BlockSpec((None, L, tc), lambda l, o, j: (l, 0, o * nct + j)),
                   pl.BlockSpec((None, 8, tc), lambda l, o, j: (l, 0, o * nct + j))),
        compiler_params=_cparams(3),
        name="hyena_filter_taps",
    )(z, hid, w3, w3, deltas)

    cm, sm = _dft_mats(L)
    tf = min(L, 512)
    spec = jax.ShapeDtypeStruct((DEPTH, L, nw), F32)
    return pl.pallas_call(
        _filt_spec_kernel,
        out_shape=(spec, spec, spec),
        grid=(L // tf, DEPTH),
        in_specs=[
            pl.BlockSpec((tf, L), lambda i, l: (i, 0)),
            pl.BlockSpec((tf, L), lambda i, l: (i, 0)),
            pl.BlockSpec((None, L, nw), lambda i, l: (l, 0, 0)),
            pl.BlockSpec((None, L, nw), lambda i, l: (l, 0, 0)),
            pl.BlockSpec((None, 8, nw), lambda i, l: (l, 0, 0)),
        ],
        out_specs=(pl.BlockSpec((None, tf, nw), lambda i, l: (l, i, 0)),) * 3,
        compiler_params=_cparams(2),
        name="hyena_filter_spectrum",
    )(cm, sm, fsum, fdiff, nyq)


def _short_conv_kernel(L, za_ref, zb_ref, zc_ref, w_ref, b_ref, o_ref):
    W = HY_W
    for j, z_ref in enumerate((za_ref, zb_ref, zc_ref)):
        z = z_ref[...]
        w = w_ref[:, j * W:(j + 1) * W]
        row = lax.broadcasted_iota(jnp.int32, z.shape, 0)
        prev = jnp.where(row == 0, 0.0, pltpu.roll(z, 1, 0))
        nxt = jnp.where(row == L - 1, 0.0, pltpu.roll(z, L - 1, 0))
        o_ref[:, j * W:(j + 1) * W] = (b_ref[:, j * W:(j + 1) * W] + prev * w[0:1] + z * w[1:2]
                                       + nxt * w[2:3])


def _short_conv(z, conv_w, conv_b):
    B, L, _ = z.shape
    W = HY_W
    nb = (HY_ORDER + 1)
    cb = Z_HY // W
    return pl.pallas_call(
        functools.partial(_short_conv_kernel, L),
        out_shape=jax.ShapeDtypeStruct((B, L, nb * W), F32),
        grid=(B,),
        in_specs=[pl.BlockSpec((None, L, W), lambda b, j=j: (b, 0, cb + j)) for j in range(nb)]
        + [pl.BlockSpec((3, nb * W), lambda b: (0, 0)),
           pl.BlockSpec((1, nb * W), lambda b: (0, 0))],
        out_specs=pl.BlockSpec((None, L, nb * W), lambda b: (b, 0, 0)),
        compiler_params=_cparams(1),
        name="hyena_short_conv",
    )(z, z, z, conv_w, conv_b.reshape(1, nb * W))


def _conv_fwd_kernel(u_ref, c_ref, s_ref, a_ref, b_ref, d_ref, yre_ref, yim_ref):
    u = u_ref[...]
    ure = _dot(c_ref[...], u)
    uim = -_dot(s_ref[...], u)
    rowu = lax.broadcasted_iota(jnp.int32, u.shape, 0)
    nyq = jnp.sum(jnp.where((rowu & 1) == 0, u, -u), axis=0, keepdims=True)
    row = lax.broadcasted_iota(jnp.int32, ure.shape, 0) + pl.program_id(0) * ure.shape[0]
    uim = jnp.where(row == 0, nyq, uim)
    a = a_ref[...]
    bm = b_ref[...]
    yre_ref[...] = (ure * a - uim * bm).astype(BF16)
    yim_ref[...] = (ure * bm + uim * d_ref[...]).astype(BF16)


def _conv_inv_kernel(L, yre_ref, yim_ref, c_ref, s_ref, u_ref, skip_ref, m_ref, o_ref):
    yre = yre_ref[...]
    yim = yim_ref[...]
    y = (jnp.dot(c_ref[...], yre, preferred_element_type=F32)
         - jnp.dot(s_ref[...], yim, preferred_element_type=F32)) * (1.0 / L)
    row = lax.broadcasted_iota(jnp.int32, y.shape, 0) + pl.program_id(0) * y.shape[0]
    sgn = jnp.where((row & 1) == 0, 1.0, -1.0)
    y = y + (sgn * yim[0:1].astype(F32) - yre[0:1].astype(F32)) * (0.5 / L)
    o_ref[...] = m_ref[...] * (y + u_ref[...] * skip_ref[...])


def _long_conv(zc, u_col, m_col, filt, layer, order, skip, u_arr=None):
    B, L, _ = zc.shape
    W = HY_W
    cm, sm = _dft_mats(L)
    fa, fb, fd = filt
    tf = min(L, 512)
    if u_arr is None:
        u_arr, ub = zc, u_col
    else:
        ub = 0
    spec_t = pl.BlockSpec((tf, L), lambda i, b: (i, 0))
    fspec = pl.BlockSpec((None, tf, W), lambda i, b: (layer, i, order))
    yre, yim = pl.pallas_call(
        _conv_fwd_kernel,
        out_shape=(jax.ShapeDtypeStruct((B, L, W), BF16),) * 2,
        grid=(L // tf, B),
        in_specs=[pl.BlockSpec((None, L, W), lambda i, b: (b, 0, ub)),
                  spec_t, spec_t, fspec, fspec, fspec],
        out_specs=(pl.BlockSpec((None, tf, W), lambda i, b: (b, i, 0)),) * 2,
        compiler_params=_cparams(2),
        name="hyena_conv_fwd",
    )(u_arr, cm, sm, fa, fb, fd)
    full = pl.BlockSpec((None, L, W), lambda i, b: (b, 0, 0))
    return pl.pallas_call(
        functools.partial(_conv_inv_kernel, L),
        out_shape=jax.ShapeDtypeStruct((B, L, W), F32),
        grid=(L // tf, B),
        in_specs=[full, full, spec_t, spec_t,
                  pl.BlockSpec((None, tf, W), lambda i, b: (b, i, ub)),
                  pl.BlockSpec((None, 1, W), lambda i, b: (order, 0, 0)),
                  pl.BlockSpec((None, tf, W), lambda i, b: (b, i, m_col))],
        out_specs=pl.BlockSpec((None, tf, W), lambda i, b: (b, i, 0)),
        compiler_params=_cparams(2),
        name="hyena_conv_inv",
    )(yre, yim, cm, sm, u_arr, skip.reshape(HY_ORDER, 1, W), zc)


def _hyena(z, filt, layer, conv_w, conv_b, skip):
    zc = _short_conv(z, conv_w, conv_b)
    y1 = _long_conv(zc, 0, 1, filt, layer, 0, skip)
    return _long_conv(zc, 0, 2, filt, layer, 1, skip, u_arr=y1)


def _out_kernel(cond_base, cond_stride, hg_ref, da_ref, hy_ref, x_ref, mod_ref, g_ref, w_ref, wr_ref,
                x1_ref, h2_ref, aff_ref):
    row = cond_base + cond_stride * pl.program_id(0)
    D = D_MODEL
    gate1 = mod_ref[pl.ds(row, 1), pl.ds(2 * D, D)]
    shift2 = mod_ref[pl.ds(row, 1), pl.ds(3 * D, D)]
    scale2 = mod_ref[pl.ds(row, 1), pl.ds(4 * D, D)]
    mix = (_dot(hg_ref[...], w_ref[0:HG_W, :])
           + _dot(da_ref[...], w_ref[HG_W:HG_W + DA_W, :])
           + _dot(hy_ref[...], w_ref[HG_W + DA_W:D, :]))
    x1 = x_ref[...] + gate1 * mix
    x1_ref[...] = x1
    h2 = _rms(x1) * g_ref[...] * (1.0 + scale2) + shift2
    h2_ref[...] = h2.astype(BF16)
    wr = wr_ref[...]
    h_hi = h2.astype(BF16)
    w_hi = wr.astype(BF16)
    h_lo = (h2 - h_hi.astype(F32)).astype(BF16)
    w_lo = (wr - w_hi.astype(F32)).astype(BF16)
    logits = (jnp.dot(h_hi, w_hi, preferred_element_type=F32) + jnp.dot(h_hi, w_lo, preferred_element_type=F32)
              + jnp.dot(h_lo, w_hi, preferred_element_type=F32))
    lane = lax.broadcasted_iota(jnp.int32, logits.shape, 1)
    logits = jnp.where(lane < N_EXPERTS, logits, -jnp.inf)
    e = jnp.exp(logits - jnp.max(logits, axis=1, keepdims=True))
    aff_ref[...] = e / jnp.sum(e, axis=1, keepdims=True)


def _out_proj(o_hg, o_da, o_hy, x, mod_l, g, w_bf, wr_pad, cond_base, cond_stride):
    B, L, D = x.shape
    tm = min(L, 512)

    def rows(w):
        return pl.BlockSpec((None, tm, w), lambda b, i: (b, i, 0))

    return pl.pallas_call(
        functools.partial(_out_kernel, cond_base, cond_stride),
        out_shape=(jax.ShapeDtypeStruct((B, L, D), F32),
                   jax.ShapeDtypeStruct((B, L, D), BF16),
                   jax.ShapeDtypeStruct((B, L, LANES), F32)),
        grid=(B, L // tm),
        in_specs=[rows(HG_W), rows(DA_W), rows(HY_W), rows(D),
                  pl.BlockSpec((8, N_MOD * D), lambda b, i: (0, 0)),
                  pl.BlockSpec((1, D), lambda b, i: (0, 0)),
                  pl.BlockSpec((D, D), lambda b, i: (0, 0)),
                  pl.BlockSpec((D, LANES), lambda b, i: (0, 0))],
        out_specs=(rows(D), rows(D), rows(LANES)),
        compiler_params=_cparams(2),
        name="out_proj",
    )(o_hg, o_da, o_hy, x, mod_l, g, w_bf, wr_pad)


def _prefix_count(m, blk):
    L = m.shape[0]
    r = lax.broadcasted_iota(jnp.int32, (blk, blk), 0)
    c = lax.broadcasted_iota(jnp.int32, (blk, blk), 1)
    tri = jnp.where(c < r, 1.0, 0.0).astype(BF16)
    carry = jnp.zeros((1, m.shape[1]), F32)
    parts = []
    for i in range(L // blk):
        mb = m[i * blk:(i + 1) * blk]
        parts.append(jnp.dot(tri, mb.astype(BF16), preferred_element_type=F32) + carry)
        carry = carry + jnp.sum(mb, axis=0, keepdims=True)
    return jnp.concatenate(parts, axis=0) if len(parts) > 1 else parts[0]


def _route_kernel(cap, aff_ref, slot_ref, slot_t_ref):
    aff = aff_ref[...]
    capf = float(cap)
    tiny = 2.0 ** -126

    def enough(t):
        return jnp.sum(jnp.where(aff >= t, 1.0, 0.0), axis=0, keepdims=True) >= capf

    has = enough(jnp.full((1, LANES), tiny, F32))
    lo = jnp.full((1, LANES), tiny, F32)
    for s in (64, 32, 16, 8, 4, 2, 1):
        cand = lo * (2.0 ** s)
        lo = jnp.where(enough(cand), cand, lo)
    step = lo
    for _ in range(23):
        step = step * 0.5
        cand = lo + step
        lo = jnp.where(enough(cand), cand, lo)
    hi = jnp.where(has, lo + jnp.maximum(step, tiny), tiny)
    lo = jnp.where(has, lo, 0.0)
    gt = jnp.where(aff >= hi, 1.0, 0.0)
    eq = jnp.where((aff >= lo) & (aff < hi), 1.0, 0.0)
    need = capf - jnp.sum(gt, axis=0, keepdims=True)
    blk = min(aff.shape[0], 256)
    sel = gt + eq * jnp.where(_prefix_count(eq, blk) < need, 1.0, 0.0)
    lane = lax.broadcasted_iota(jnp.int32, aff.shape, 1)
    sel = jnp.where(lane < N_EXPERTS, sel, 0.0)
    slot = jnp.where(sel > 0.0, _prefix_count(sel, blk), -1.0)
    slot_ref[...] = slot
    slot_t_ref[...] = jnp.transpose(slot)[0:N_EXPERTS, :]


def _route(aff, cap):
    B, L, _ = aff.shape
    return pl.pallas_call(
        functools.partial(_route_kernel, cap),
        out_shape=(jax.ShapeDtypeStruct((B, L, LANES), F32),
                   jax.ShapeDtypeStruct((B, N_EXPERTS, L), F32)),
        grid=(B,),
        in_specs=[pl.BlockSpec((None, L, LANES), lambda b: (b, 0, 0))],
        out_specs=(pl.BlockSpec((None, L, LANES), lambda b: (b, 0, 0)),
                   pl.BlockSpec((None, N_EXPERTS, L), lambda b: (b, 0, 0))),
        compiler_params=_cparams(1),
        name="route",
    )(aff)


def _gather_kernel(cap, eg, slot_t_ref, h_ref, xs_ref):
    e0 = pl.program_id(1) * eg
    c = lax.broadcasted_iota(jnp.int32, (cap, slot_t_ref.shape[1]), 0).astype(F32)
    onehot = jnp.concatenate(
        [jnp.where(c == slot_t_ref[pl.ds(e0 + j, 1), :], 1.0, 0.0) for j in range(eg)],
        axis=0).astype(BF16)
    xs = jnp.dot(onehot, h_ref[...], preferred_element_type=F32).astype(BF16)
    for j in range(eg):
        xs_ref[j] = xs[j * cap:(j + 1) * cap]


def _gather(slot_t, h2, cap):
    B, L, D = h2.shape
    eg = max(1, min(N_EXPERTS, GATHER_ROWS // cap))
    return pl.pallas_call(
        functools.partial(_gather_kernel, cap, eg),
        out_shape=jax.ShapeDtypeStruct((N_EXPERTS, B * cap, D), BF16),
        grid=(B, N_EXPERTS // eg),
        in_specs=[pl.BlockSpec((None, N_EXPERTS, L), lambda b, e: (b, 0, 0)),
                  pl.BlockSpec((None, L, D), lambda b, e: (b, 0, 0))],
        out_specs=pl.BlockSpec((eg, cap, D), lambda b, e: (e, b, 0)),
        compiler_params=_cparams(2),
        name="expert_gather",
    )(slot_t, h2)


def _ffn_kernel(n_f, xa_ref, xb_ref, wg_ref, wu_ref, wd_ref, oa_ref, ob_ref, acc_ref):
    f = pl.program_id(1)
    wg = wg_ref[...].astype(BF16)
    wu = wu_ref[...].astype(BF16)
    wd = wd_ref[...].astype(BF16)
    for i, x_ref in enumerate((xa_ref, xb_ref)):
        x = x_ref[...]
        hid = (_silu(jnp.dot(x, wg, preferred_element_type=F32))
               * jnp.dot(x, wu, preferred_element_type=F32))
        part = jnp.dot(hid.astype(BF16), wd, preferred_element_type=F32)

        @pl.when(f == 0)
        def _():
            acc_ref[i] = part

        @pl.when(f > 0)
        def _():
            acc_ref[i] = acc_ref[i] + part

    @pl.when(f == n_f - 1)
    def _():
        oa_ref[...] = acc_ref[0].astype(BF16)
        ob_ref[...] = acc_ref[1].astype(BF16)


def _expert_ffn(xs_a, xs_b, w_gate, w_up, w_down):
    E, S, D = xs_a.shape
    tf = 512
    n_f = D_FF // tf
    xspec = pl.BlockSpec((None, S, D), lambda e, f: (e, 0, 0))
    out = jax.ShapeDtypeStruct((E, S, D), BF16)
    return pl.pallas_call(
        functools.partial(_ffn_kernel, n_f),
        out_shape=(out, out),
        grid=(E, n_f),
        in_specs=[xspec, xspec,
                  pl.BlockSpec((None, D, tf), lambda e, f: (e, 0, f)),
                  pl.BlockSpec((None, D, tf), lambda e, f: (e, 0, f)),
                  pl.BlockSpec((None, tf, D), lambda e, f: (e, f, 0))],
        out_specs=(xspec, xspec),
        scratch_shapes=[pltpu.VMEM((2, S, D), F32)],
        compiler_params=_cparams(2),
        name="expert_ffn",
    )(xs_a, xs_b, w_gate, w_up, w_down)


def _scatter_kernel(cap, cond_base, cond_stride, final, ys_ref, slot_ref, aff_ref, x_ref, mod_ref, fg_ref,
                    o_ref):
    row = cond_base + cond_stride * pl.program_id(0)
    gate2 = mod_ref[pl.ds(row, 1), pl.ds(5 * D_MODEL, D_MODEL)]
    slot = slot_ref[...]
    aff = aff_ref[...]
    tt = slot.shape[0]
    c = lax.broadcasted_iota(jnp.int32, (tt, cap), 1).astype(F32)
    y = jnp.zeros((tt, D_MODEL), F32)
    for e in range(N_EXPERTS):
        onehot = jnp.where(c == slot[:, e:e + 1], 1.0, 0.0).astype(BF16)
        y = y + aff[:, e:e + 1] * jnp.dot(onehot, ys_ref[e], preferred_element_type=F32)
    x2 = x_ref[...] + gate2 * y
    if final:
        x2 = _rms(x2) * fg_ref[...]
    o_ref[...] = x2


def _scatter(ys, slot, aff, x1, mod_l, fg, cap, cond_base, cond_stride, final):
    B, L, D = x1.shape
    tt = min(L, 512)

    def rows(w):
        return pl.BlockSpec((None, tt, w), lambda b, i: (b, i, 0))

    return pl.pallas_call(
        functools.partial(_scatter_kernel, cap, cond_base, cond_stride, final),
        out_shape=jax.ShapeDtypeStruct((B, L, D), F32),
        grid=(B, L // tt),
        in_specs=[pl.BlockSpec((N_EXPERTS, cap, D), lambda b, i: (0, b, 0)),
                  rows(LANES), rows(LANES), rows(D),
                  pl.BlockSpec((8, N_MOD * D), lambda b, i: (0, 0)),
                  pl.BlockSpec((1, D), lambda b, i: (0, 0))],
        out_specs=rows(D),
        compiler_params=_cparams(2),
        name="expert_scatter",
    )(ys, slot, aff, x1, mod_l, fg)


def _state_to_kernel_layout(s):
    B = s.shape[0]
    st = jnp.swapaxes(s, -1, -2).reshape(B, 2, 2, 2, HG_DK, HG_DK)
    eye = jnp.eye(2, dtype=s.dtype)
    full = st[:, :, :, :, :, None, :] * eye[None, None, None, :, None, :, None]
    return full.reshape(B, 2, 2, 2 * HG_DK, 2 * HG_DK)


def _state_from_kernel_layout(st):
    B = st.shape[0]
    full = st.reshape(B, 2, 2, 2, HG_DK, 2, HG_DK)
    diag = jnp.stack([full[:, :, :, h, :, h, :] for h in range(2)], axis=3)
    return jnp.swapaxes(diag.reshape(B, 2, HG_HEADS, HG_DK, HG_DK), -1, -2)


def kernel(x_prompt, x_sample, c, cache_diff_k, cache_diff_v, state_hgrn, c_ctx, norm_g, final_norm_g,
           w_mod, b_mod, w_in, w_out, hgrn_lb_logits, hgrn_norm_g, diff_lambda, diff_subln_g,
           hy_conv_w, hy_conv_b, hy_filt_w1, hy_filt_b1, hy_filt_w2, hy_filt_b2, hy_filt_w3, hy_filt_freq,
           hy_skip, w_router, w_gate, w_up, w_down):
    B_p, L_p, D = x_prompt.shape
    B_s, L_s, _ = x_sample.shape
    cap_p = EC_CAPACITY_FACTOR * L_p // N_EXPERTS
    cap_s = EC_CAPACITY_FACTOR * L_s // N_EXPERTS
    assert B_p * cap_p == B_s * cap_s

    cond8 = jnp.zeros((8, D), F32).at[0].set(c_ctx.astype(F32)).at[1:1 + B_s].set(c.astype(F32))
    mod = _modulation(cond8, w_mod, b_mod)
    groups = (
        dict(cond=(0, 0), cap=cap_p, L=L_p),
        dict(cond=(1, 1), cap=cap_s, L=L_s),
    )
    filt = {L: _hyena_filters(L, hy_filt_w1, hy_filt_b1, hy_filt_w2, hy_filt_b2, hy_filt_w3, hy_filt_freq)
            for L in {L_p, L_s}}
    rope_tabs = _rope_tables(L_s)
    fg = final_norm_g.astype(F32).reshape(1, D)
    wr_pad = jnp.pad(w_router.astype(F32), ((0, 0), (0, 0), (0, LANES - N_EXPERTS)))
    s0 = jnp.zeros((B_p, 2, 2, HG_W // 2, HG_W // 2), F32)
    caches = None

    xs = [x_prompt.astype(F32), x_sample.astype(F32)]
    new_s = []
    for l in range(DEPTH):
        w_in_bf = w_in[l].astype(BF16)
        w_out_bf = w_out[l].astype(BF16)
        ng = jnp.tile(hgrn_norm_g[l].astype(F32), HG_HEADS).reshape(1, HG_W)
        sg = diff_subln_g[l].astype(F32).reshape(1, DA_DV)
        g1 = norm_g[l, 0].astype(F32).reshape(1, D)
        g2 = norm_g[l, 1].astype(F32).reshape(1, D)
        x1s, slots, affs, gathered = [], [], [], []
        for gi, grp in enumerate(groups):
            x = xs[gi]
            cb, cs = grp["cond"]
            if gi == 0:
                z, qa, ka, va, kc, vc = _in_proj(x, mod[l], g1, w_in_bf, cb, cs, l, caches=caches,
                                                 cache_dtype=x_prompt.dtype)
                caches = (kc, vc)
                s_init = s0
            else:
                z, qa, ka, va = _in_proj(x, mod[l], g1, w_in_bf, cb, cs, l, rope_tabs=rope_tabs)
                s_init = _state_to_kernel_layout(state_hgrn[:, l].astype(F32))
            o_hg, s_fin = _hgrn(z, hgrn_lb_logits.astype(F32), s_init, ng, l)
            if gi == 0:
                new_s.append(_state_from_kernel_layout(s_fin))
                o_da = _attention(qa, ka, va, l, diff_lambda[l].astype(F32), sg)
            else:
                o_da = _attention(qa, ka, va, l, diff_lambda[l].astype(F32), sg,
                                  cache_diff_k.astype(F32), cache_diff_v.astype(F32))
            o_hy = _hyena(z, filt[grp["L"]], l, hy_conv_w[l].astype(F32), hy_conv_b[l].astype(F32),
                          hy_skip[l].astype(F32))
            x1, h2, aff = _out_proj(o_hg, o_da, o_hy, x, mod[l], g2, w_out_bf, wr_pad[l], cb, cs)
            slot, slot_t = _route(aff, grp["cap"])
            x1s.append(x1)
            slots.append(slot)
            affs.append(aff)
            gathered.append(_gather(slot_t, h2, grp["cap"]))
        ys = _expert_ffn(gathered[0], gathered[1], w_gate[l], w_up[l], w_down[l])
        for gi, grp in enumerate(groups):
            cb, cs = grp["cond"]
            xs[gi] = _scatter(ys[gi], slots[gi], affs[gi], x1s[gi], mod[l], fg, grp["cap"], cb, cs,
                              final=(l == DEPTH - 1))

    dt = x_prompt.dtype
    return (xs[0].astype(dt), xs[1].astype(x_sample.dtype),
            caches[0], caches[1], jnp.stack(new_s, axis=1).astype(dt))
```

```python
import functools
import math

import jax
import jax.numpy as jnp
import numpy as np
from jax import lax
from jax.experimental import pallas as pl
from jax.experimental.pallas import tpu as pltpu

F32 = jnp.float32
BF16 = jnp.bfloat16
HI = lax.Precision.HIGHEST

D_MODEL = 1024
DEPTH = 4
GRID_W = 64
HG_HEADS = 4
HG_DK = 64
HG_W = 256
DA_HEADS = 4
DA_DQK = 64
DA_DV = 128
DA_W = 512
HY_W = 256
HY_ORDER = 2
HY_POS_EMB = 33
HY_HIDDEN = 64
HY_SHORT_DECAY_PCT = 0.3
HY_LONG_DECAY_PCT = 1.5
HY_TARGET = 1e-2
N_EXPERTS = 16
EC_CAPACITY_FACTOR = 2
D_FF = 1024
N_MOD = 6
ROPE_BASE = 10000.0
NORM_EPS = 1e-6
D_IN = 3584
COL_HQ, COL_HFF, COL_HFB, COL_HI, COL_HG = 0, 256, 512, 768, 1024
COL_AQ, COL_AK, COL_AV, COL_HY = 1280, 1792, 2304, 2816
Z_W = COL_AQ + (D_IN - COL_HY)
Z_HY = COL_AQ

LANES = 128
HGRN_CHUNK = 16
HGRN_GROUP = 8
GATHER_ROWS = 1024
VMEM_LIMIT = 56 * 1024 * 1024


def _cparams(n_axes):
    return pltpu.CompilerParams(dimension_semantics=("arbitrary",) * n_axes,
                                vmem_limit_bytes=VMEM_LIMIT)


def _dot(a, b):
    return jnp.dot(a.astype(BF16), b.astype(BF16), preferred_element_type=F32)


def _dot_nt(a, b):
    return lax.dot_general(a.astype(BF16), b.astype(BF16), (((1,), (1,)), ((), ())),
                           preferred_element_type=F32)


def _dot_tn(a, b):
    return lax.dot_general(a.astype(BF16), b.astype(BF16), (((0,), (0,)), ((), ())),
                           preferred_element_type=F32)


def _dot_hi(a, b):
    return jnp.dot(a, b, precision=HI, preferred_element_type=F32)


def _sigmoid(x):
    return 1.0 / (1.0 + jnp.exp(-x))


def _silu(x):
    return x * _sigmoid(x)


def _mod_kernel(cond_ref, w_ref, b_ref, o_ref):
    s = _silu(cond_ref[...])
    o_ref[...] = _dot(s, w_ref[...]) + b_ref[...]


def _modulation(cond8, w_mod, b_mod):
    tn = 1536
    n_mod = N_MOD * D_MODEL
    return pl.pallas_call(
        _mod_kernel,
        out_shape=jax.ShapeDtypeStruct((DEPTH, 8, n_mod), F32),
        grid=(DEPTH, n_mod // tn),
        in_specs=[
            pl.BlockSpec((8, D_MODEL), lambda l, j: (0, 0)),
            pl.BlockSpec((None, D_MODEL, tn), lambda l, j: (l, 0, j)),
            pl.BlockSpec((None, 1, tn), lambda l, j: (l, 0, j)),
        ],
        out_specs=pl.BlockSpec((None, 8, tn), lambda l, j: (l, 0, j)),
        compiler_params=_cparams(2),
        name="modulation",
    )(cond8, w_mod, b_mod.reshape(DEPTH, 1, n_mod))


def _rms(x):
    return x * lax.rsqrt(jnp.mean(x * x, axis=-1, keepdims=True) + NORM_EPS)


def _rope(x, cos, sin):
    lane = lax.broadcasted_iota(jnp.int32, x.shape, 1)
    swapped = jnp.where((lane & 31) < 16, -pltpu.roll(x, LANES - 16, 1), pltpu.roll(x, 16, 1))
    return x * cos + swapped * sin


def _in_kernel(cond_base, cond_stride, latent, x_ref, mod_ref, g_ref, w_ref, *rest):
    if latent:
        cos_ref, sin_ref, z_ref, q_ref, k_ref, v_ref = rest
    else:
        z_ref, q_ref, k_ref, v_ref, kc_ref, vc_ref = rest[-6:]
    row = cond_base + cond_stride * pl.program_id(0)
    shift = mod_ref[pl.ds(row, 1), pl.ds(0, D_MODEL)]
    scale = mod_ref[pl.ds(row, 1), pl.ds(D_MODEL, D_MODEL)]
    h = (_rms(x_ref[...]) * g_ref[...] * (1.0 + scale) + shift).astype(BF16)
    z_ref[:, 0:COL_AQ] = jnp.dot(h, w_ref[:, 0:COL_AQ], preferred_element_type=F32)
    z_ref[:, COL_AQ:Z_W] = jnp.dot(h, w_ref[:, COL_HY:D_IN], preferred_element_type=F32)
    att = jnp.dot(h, w_ref[:, COL_AQ:COL_HY], preferred_element_type=F32)
    q = att[:, 0:DA_W]
    k = att[:, DA_W:2 * DA_W]
    v = att[:, 2 * DA_W:3 * DA_W]
    if latent:
        cos = cos_ref[...]
        sin = sin_ref[...]
        heads = [slice(hh * LANES, (hh + 1) * LANES) for hh in range(DA_HEADS)]
        q = jnp.concatenate([_rope(q[:, s], cos, sin) for s in heads], axis=1)
        k = jnp.concatenate([_rope(k[:, s], cos, sin) for s in heads], axis=1)
    else:
        for hh in range(DA_HEADS):
            for m in range(2):
                c0 = (2 * hh + m) * DA_DQK
                kc_ref[hh, m] = k[:, c0:c0 + DA_DQK].astype(kc_ref.dtype)
            vc_ref[hh] = v[:, hh * DA_DV:(hh + 1) * DA_DV].astype(vc_ref.dtype)
    q_ref[...] = (q * (DA_DQK ** -0.5)).astype(BF16)
    k_ref[...] = k.astype(BF16)
    v_ref[...] = v.astype(BF16)


def _in_proj(x, mod, g, w_bf, cond_base, cond_stride, layer, rope_tabs=None, caches=None, cache_dtype=None):
    B, L, _ = x.shape
    tm = 256
    latent = rope_tabs is not None
    rows = lambda w: pl.BlockSpec((None, tm, w), lambda b, i: (b, i, 0))
    in_specs = [
        rows(D_MODEL),
        pl.BlockSpec((None, 8, N_MOD * D_MODEL), lambda b, i: (layer, 0, 0)),
        pl.BlockSpec((1, D_MODEL), lambda b, i: (0, 0)),
        pl.BlockSpec((None, D_MODEL, D_IN), lambda b, i: (layer, 0, 0)),
    ]
    args = [x, mod, g, w_bf]
    out_shape = [jax.ShapeDtypeStruct((B, L, Z_W), F32)] + [jax.ShapeDtypeStruct((B, L, DA_W), BF16)] * 3
    out_specs = [rows(Z_W), rows(DA_W), rows(DA_W), rows(DA_W)]
    aliases = {}
    if latent:
        in_specs += [pl.BlockSpec((tm, LANES), lambda b, i: (i, 0))] * 2
        args += list(rope_tabs)
    else:
        assert tm == L
        out_shape += [jax.ShapeDtypeStruct((B, DEPTH, DA_HEADS, 2, L, DA_DQK), cache_dtype),
                      jax.ShapeDtypeStruct((B, DEPTH, DA_HEADS, L, DA_DV), cache_dtype)]
        out_specs += [pl.BlockSpec((None, None, DA_HEADS, 2, tm, DA_DQK), lambda b, i: (b, layer, 0, 0, i, 0)),
                      pl.BlockSpec((None, None, DA_HEADS, tm, DA_DV), lambda b, i: (b, layer, 0, i, 0))]
        if caches is not None:
            in_specs += [pl.BlockSpec(memory_space=pl.ANY)] * 2
            args += list(caches)
            aliases = {4: 4, 5: 5}
    return pl.pallas_call(
        functools.partial(_in_kernel, cond_base, cond_stride, latent),
        out_shape=tuple(out_shape),
        grid=(B, L // tm),
        in_specs=in_specs,
        out_specs=tuple(out_specs),
        input_output_aliases=aliases,
        compiler_params=_cparams(2),
        name="in_proj",
    )(*args)


def _head_block_mask(n, width):
    r = lax.broadcasted_iota(jnp.int32, (n, n), 0) // width
    c = lax.broadcasted_iota(jnp.int32, (n, n), 1) // width
    return r == c


def _hgrn_kernel(L, layer, q_ref, ff_ref, fb_ref, v_ref, g_ref, lbl_ref, s0_ref, ng_ref,
                 o_ref, sfin_ref, qh_s, vh_s, bf_s, bb_s, kf_s, kb_s, of_s, ob_s, st_s):
    C = HGRN_CHUNK
    G = HGRN_GROUP
    R = G * C
    W = HG_W
    HALF = W // 2
    ng_groups = L // R

    lg = [lbl_ref[j] for j in range(DEPTH)]
    mx = functools.reduce(jnp.maximum, lg)
    ex = [jnp.exp(a - mx) for a in lg]
    den = functools.reduce(lambda a, b: a + b, ex)
    sm = [e / den for e in ex]
    cum = sm[0]
    for j in range(1, layer + 1):
        cum = cum + sm[j]
    lb = cum - sm[0]

    rowc = lax.broadcasted_iota(jnp.int32, (L, W), 0) & (C - 1)

    def gates(f, lbd):
        e = jnp.exp(-jnp.abs(f))
        r = 1.0 / (1.0 + e)
        log_sig = jnp.minimum(f, 0.0) + jnp.log(r)
        sig_neg = jnp.where(f >= 0.0, e * r, r)
        if layer == 0:
            return sig_neg, log_sig
        a = jnp.log(lbd)
        c = jnp.log1p(-lbd) + log_sig
        logf = jnp.maximum(a, c) + jnp.log1p(jnp.exp(-jnp.abs(a - c)))
        return (1.0 - lbd) * sig_neg, logf

    kf, lf = gates(ff_ref[...], lb[0:1])
    kb, lbk = gates(fb_ref[...], lb[1:2])

    def put(ref, x):
        ref[0] = x[:, 0:HALF]
        ref[1] = x[:, HALF:W]

    def get(ref, rows):
        return jnp.concatenate([ref.at[0][rows, :], ref.at[1][rows, :]], axis=1)

    put(qh_s, q_ref[...])
    put(vh_s, v_ref[...])
    put(kf_s, kf)
    put(kb_s, kb)
    d = 1
    while d < C:
        lf = lf + jnp.where(rowc >= d, pltpu.roll(lf, d, 0), 0.0)
        lbk = lbk + jnp.where(rowc < C - d, pltpu.roll(lbk, L - d, 0), 0.0)
        d *= 2
    put(bf_s, lf)
    put(bb_s, lbk)

    st_s[...] = s0_ref[...]
    bd_bf = jnp.where(_head_block_mask(W, HG_DK), 1.0, 0.0).astype(BF16)
    bd_half = _head_block_mask(HALF, HG_DK)

    def head_sum(p):
        return jnp.dot(p.astype(BF16), bd_bf, preferred_element_type=F32)

    def group(r0, b_s, k_s, o_s, d, forward):
        def pos(ref, s):
            return get(ref, pl.ds(r0 + s, G, stride=C))

        pairs = [(t, s) for s in range(C) for t in (range(s, C) if forward else range(s + 1))]
        prods = [pos(qh_s, t) * pos(k_s, s) if t == s
                 else jnp.exp(pos(b_s, t) - pos(b_s, s)) * pos(qh_s, t) * pos(k_s, s) for t, s in pairs]
        a = head_sum(jnp.concatenate(prods, axis=0))
        acc = [None] * C
        for i, (t, s) in enumerate(pairs):
            term = a[i * G:(i + 1) * G] * pos(vh_s, s)
            acc[t] = term if acc[t] is None else acc[t] + term
        for t in range(C):
            o_s.at[0][pl.ds(r0 + t, G, stride=C), :] = acc[t][:, 0:HALF]
            o_s.at[1][pl.ds(r0 + t, G, stride=C), :] = acc[t][:, HALF:W]
        order = list(range(G)) if forward else list(range(G - 1, -1, -1))
        halves = [slice(p * HALF, (p + 1) * HALF) for p in range(2)]
        qe, kv, dec = {}, {}, {}
        for c in order:
            rows = pl.ds(r0 + c * C, C)
            b = get(b_s, rows)
            v = get(vh_s, rows)
            b_end = b[C - 1:C] if forward else b[0:1]
            qe[c] = get(qh_s, rows) * jnp.exp(b)
            ke = get(k_s, rows) * jnp.exp(b_end - b)
            dec[c] = jnp.exp(b_end)
            kv[c] = [jnp.where(bd_half, _dot_tn(v[:, ln], ke[:, ln]), 0.0) for ln in halves]
        st = [st_s[d, p] for p in range(2)]
        before = {}
        for c in order:
            before[c] = [s.astype(BF16) for s in st]
            st = [st[p] * dec[c][:, halves[p]] + kv[c][p] for p in range(2)]
        for p in range(2):
            st_s[d, p] = st[p]
        for c in order:
            rows = pl.ds(r0 + c * C, C)
            for p in range(2):
                o_s.at[p][rows, :] = o_s.at[p][rows, :] + _dot_nt(qe[c][:, halves[p]], before[c][p])

    def body(i, carry):
        group(pl.multiple_of(i * R, R), bf_s, kf_s, of_s, 0, True)
        group(pl.multiple_of((ng_groups - 1 - i) * R, R), bb_s, kb_s, ob_s, 1, False)
        return carry

    lax.fori_loop(0, ng_groups, body, 0)

    o = jnp.concatenate([of_s[0] + ob_s[0], of_s[1] + ob_s[1]], axis=1)
    ms = _dot_hi(o * o, jnp.where(_head_block_mask(W, HG_DK), 1.0 / HG_DK, 0.0).astype(F32))
    o_ref[...] = o * lax.rsqrt(ms + NORM_EPS) * ng_ref[...] * _silu(g_ref[...])
    sfin_ref[...] = st_s[...]


def _hgrn(z, lb_logits, s0t, ng_tiled, layer):
    B, L, _ = z.shape
    W = HG_W
    HALF = W // 2

    def col(c):
        return pl.BlockSpec((None, L, W), lambda b, c=c: (b, 0, c // W))

    seq = pltpu.VMEM((2, L, HALF), F32)
    st_spec = pl.BlockSpec((None, 2, 2, HALF, HALF), lambda b: (b, 0, 0, 0, 0))
    return pl.pallas_call(
        functools.partial(_hgrn_kernel, L, layer),
        out_shape=(jax.ShapeDtypeStruct((B, L, W), F32),
                   jax.ShapeDtypeStruct((B, 2, 2, HALF, HALF), F32)),
        grid=(B,),
        in_specs=[col(COL_HQ), col(COL_HFF), col(COL_HFB), col(COL_HI), col(COL_HG),
                  pl.BlockSpec((DEPTH, 2, W), lambda b: (0, 0, 0)),
                  st_spec,
                  pl.BlockSpec((1, W), lambda b: (0, 0))],
        out_specs=(pl.BlockSpec((None, L, W), lambda b: (b, 0, 0)), st_spec),
        scratch_shapes=[seq] * 8 + [pltpu.VMEM((2, 2, HALF, HALF), F32)],
        compiler_params=_cparams(1),
        name="hgrn",
    )(z, z, z, z, z, lb_logits, s0t, ng_tiled)


def _attn_kernel(latent, lam_init, n_heads, q_ref, k_ref, v_ref, *rest):
    if latent:
        ck_ref, cv_ref, lam_ref, sg_ref, o_ref = rest
    else:
        lam_ref, sg_ref, o_ref = rest
    lv = lam_ref[...]
    lam = (jnp.exp(jnp.sum(lv[0:1] * lv[1:2], axis=1, keepdims=True))
           - jnp.exp(jnp.sum(lv[2:3] * lv[3:4], axis=1, keepdims=True)) + lam_init)
    for hh in range(n_heads):
        hs = slice(hh * LANES, (hh + 1) * LANES)
        q = q_ref[:, hs]
        k = k_ref[:, hs]
        v = v_ref[:, hs]
        lane = lax.broadcasted_iota(jnp.int32, q.shape, 1)
        outs = []
        for m in range(2):
            qm = jnp.where((lane >= m * DA_DQK) & (lane < (m + 1) * DA_DQK), q, jnp.zeros_like(q))
            s = lax.dot_general(qm, k, (((1,), (1,)), ((), ())), preferred_element_type=F32)
            mx = jnp.max(s, axis=1, keepdims=True)
            if latent:
                sc = _dot_nt(q[:, m * DA_DQK:(m + 1) * DA_DQK], ck_ref[hh, m])
                mx = jnp.maximum(mx, jnp.max(sc, axis=1, keepdims=True))
            p = jnp.exp(s - mx)
            den = jnp.sum(p, axis=1, keepdims=True)
            o = jnp.dot(p.astype(BF16), v, preferred_element_type=F32)
            if latent:
                pc = jnp.exp(sc - mx)
                den = den + jnp.sum(pc, axis=1, keepdims=True)
                o = o + _dot(pc, cv_ref[hh])
            outs.append(o / den)
        o = outs[0] - lam * outs[1]
        o_ref[:, hs] = _rms(o) * sg_ref[...] * (1.0 - lam_init)


def _attention(q, k, v, layer, lam_p, subln_g, ctx_k=None, ctx_v=None):
    B, L, _ = q.shape
    latent = ctx_k is not None
    lam_init = 0.8 - 0.6 * math.exp(-0.3 * layer)
    if latent:
        tq, nh = 256, 1
        grid = (B, DA_HEADS, L // tq)
        P = ctx_k.shape[4]
        in_specs = [
            pl.BlockSpec((None, tq, LANES), lambda b, h, i: (b, i, h)),
            pl.BlockSpec((None, L, LANES), lambda b, h, i: (b, 0, h)),
            pl.BlockSpec((None, L, LANES), lambda b, h, i: (b, 0, h)),
            pl.BlockSpec((None, None, 1, 2, P, DA_DQK), lambda b, h, i: (b, layer, h, 0, 0, 0)),
            pl.BlockSpec((None, None, 1, P, DA_DV), lambda b, h, i: (b, layer, h, 0, 0)),
            pl.BlockSpec((4, DA_DQK), lambda b, h, i: (0, 0)),
            pl.BlockSpec((1, DA_DV), lambda b, h, i: (0, 0)),
        ]
        args = [q, k, v, ctx_k, ctx_v, lam_p, subln_g]
        out_spec = pl.BlockSpec((None, tq, DA_DV), lambda b, h, i: (b, i, h))
    else:
        nh = DA_HEADS
        grid = (B,)
        full = pl.BlockSpec((None, L, DA_W), lambda b: (b, 0, 0))
        in_specs = [full, full, full,
                    pl.BlockSpec((4, DA_DQK), lambda b: (0, 0)),
                    pl.BlockSpec((1, DA_DV), lambda b: (0, 0))]
        args = [q, k, v, lam_p, subln_g]
        out_spec = full
    return pl.pallas_call(
        functools.partial(_attn_kernel, latent, lam_init, nh),
        out_shape=jax.ShapeDtypeStruct((B, L, DA_W), F32),
        grid=grid,
        in_specs=in_specs,
        out_specs=out_spec,
        compiler_params=_cparams(len(grid)),
        name="diff_attention",
    )(*args)


def _rope_tables(L):
    half = DA_DQK // 2
    inv = ROPE_BASE ** (-jnp.arange(0, half, 2, dtype=F32) / half)
    t = jnp.arange(L)
    rows = (t // GRID_W).astype(F32)[:, None] * inv[None, :]
    cols = (t % GRID_W).astype(F32)[:, None] * inv[None, :]
    ang = jnp.concatenate([rows, rows, cols, cols], axis=-1)
    ang = jnp.concatenate([ang, ang], axis=-1)
    return jnp.cos(ang), jnp.sin(ang)


@functools.lru_cache(maxsize=None)
def _dft_np(L):
    idx = (np.arange(L, dtype=np.int64)[:, None] * np.arange(L, dtype=np.int64)[None, :]) % (2 * L)
    ang = idx.astype(np.float64) * (math.pi / L)
    return np.cos(ang).astype(np.float32), np.sin(ang).astype(np.float32)


def _dft_mats(L):
    c, s = _dft_np(L)
    return jnp.asarray(c).astype(BF16), jnp.asarray(s).astype(BF16)


def _filt_hidden_kernel(z_ref, w1_ref, b1_ref, w2_ref, b2_ref, fr_ref, hid_ref):
    fr = fr_ref[...]
    hid = jnp.sin(fr[0:1] * (_dot_hi(z_ref[...], w1_ref[...]) + b1_ref[...]))
    hid_ref[...] = jnp.sin(fr[1:2] * (_dot_hi(hid, w2_ref[...]) + b2_ref[...]))


def _filt_taps_kernel(z_ref, hid_ref, w3f_ref, w3b_ref, dl_ref, sum_ref, diff_ref, nyq_ref):
    hid = hid_ref[...]
    decay = jnp.exp(-z_ref[:, 0:1] * jnp.abs(dl_ref[...]))
    hf = _dot_hi(hid, w3f_ref[...]) * decay
    hb = _dot_hi(hid, w3b_ref[...]) * decay
    nrm = (jnp.sum(jnp.abs(hf), axis=0, keepdims=True)
           + jnp.sum(jnp.abs(hb), axis=0, keepdims=True))
    row = lax.broadcasted_iota(jnp.int32, hf.shape, 0)
    hf = hf / nrm
    hb = jnp.where(row == 0, 0.0, hb / nrm)
    fs = hf + hb
    sum_ref[...] = fs
    diff_ref[...] = hf - hb
    sgn = jnp.where((row & 1) == 0, 1.0, -1.0)
    nyq_ref[...] = jnp.broadcast_to(jnp.sum(fs * sgn, axis=0, keepdims=True), nyq_ref.shape)


def _filt_spec_kernel(c_ref, s_ref, sum_ref, diff_ref, nyq_ref, a_ref, b_ref, d_ref):
    a = _dot(c_ref[...], sum_ref[...])
    him = -_dot(s_ref[...], diff_ref[...])
    row = lax.broadcasted_iota(jnp.int32, a.shape, 0) + pl.program_id(0) * a.shape[0]
    a_ref[...] = a
    b_ref[...] = jnp.where(row == 0, 0.0, him)
    d_ref[...] = jnp.where(row == 0, nyq_ref[0:1], a)


def _hyena_filters(L, w1, b1, w2, b2, w3, freq):
    t = jnp.linspace(0.0, 1.0, L, dtype=F32)[:, None]
    bands = (HY_POS_EMB - 1) // 2
    fr = jnp.linspace(1e-4, bands - 1, bands, dtype=F32)[None, :]
    w = 2.0 * math.pi * jnp.arange(L, dtype=F32)[:, None] / L
    z = jnp.concatenate([t, jnp.cos(fr * w), -jnp.sin(fr * w)], axis=-1)
    z = jnp.pad(z, ((0, 0), (0, HY_HIDDEN - HY_POS_EMB)))
    w1p = jnp.pad(w1, ((0, 0), (0, HY_HIDDEN - HY_POS_EMB), (0, 0)))
    max_decay = math.log(HY_TARGET) / HY_SHORT_DECAY_PCT
    min_decay = math.log(HY_TARGET) / HY_LONG_DECAY_PCT
    deltas = jnp.linspace(min_decay, max_decay, HY_W, dtype=F32)[None, :]
    tc = LANES
    nct = HY_W // tc
    nw = HY_ORDER * HY_W
    hid = pl.pallas_call(
        _filt_hidden_kernel,
        out_shape=jax.ShapeDtypeStruct((DEPTH, L, HY_HIDDEN), F32),
        grid=(DEPTH,),
        in_specs=[
            pl.BlockSpec((L, HY_HIDDEN), lambda l: (0, 0)),
            pl.BlockSpec((None, HY_HIDDEN, HY_HIDDEN), lambda l: (l, 0, 0)),
            pl.BlockSpec((None, 1, HY_HIDDEN), lambda l: (l, 0, 0)),
            pl.BlockSpec((None, HY_HIDDEN, HY_HIDDEN), lambda l: (l, 0, 0)),
            pl.BlockSpec((None, 1, HY_HIDDEN), lambda l: (l, 0, 0)),
            pl.BlockSpec((None, 2, HY_HIDDEN), lambda l: (l, 0, 0)),
        ],
        out_specs=pl.BlockSpec((None, L, HY_HIDDEN), lambda l: (l, 0, 0)),
        compiler_params=_cparams(1),
        name="hyena_filter_hidden",
    )(z, w1p, b1.reshape(DEPTH, 1, HY_HIDDEN), w2, b2.reshape(DEPTH, 1, HY_HIDDEN), freq)
    fsum, fdiff, nyq = pl.pallas_call(
        _filt_taps_kernel,
        out_shape=(jax.ShapeDtypeStruct((DEPTH, L, nw), F32),
                   jax.ShapeDtypeStruct((DEPTH, L, nw), F32),
                   jax.ShapeDtypeStruct((DEPTH, 8, nw), F32)),
        grid=(DEPTH, HY_ORDER, nct),
        in_specs=[
            pl.BlockSpec((L, HY_HIDDEN), lambda l, o, j: (0, 0)),
            pl.BlockSpec((None, L, HY_HIDDEN), lambda l, o, j: (l, 0, 0)),
            pl.BlockSpec((None, HY_HIDDEN, tc), lambda l, o, j: (l, 0, o * nct + j)),
            pl.BlockSpec((None, HY_HIDDEN, tc), lambda l, o, j: (l, 0, (HY_ORDER + o) * nct + j)),
            pl.BlockSpec((1, tc), lambda l, o, j: (0, j)),
        ],
        out_specs=(pl.BlockSpec((None, L, tc), lambda l, o, j: (l, 0, o * nct + j)),
                   pl.BlockSpec((None, L, tc), lambda l, o, j: (l, 0, o * nct + j)),
                   pl.BlockSpec((None, 8, tc), lambda l, o, j: (l, 0, o * nct + j))),
        compiler_params=_cparams(3),
        name="hyena_filter_taps",
    )(z, hid, w3, w3, deltas)

    cm, sm = _dft_mats(L)
    tf = min(L, 512)
    spec = jax.ShapeDtypeStruct((DEPTH, L, nw), F32)
    return pl.pallas_call(
        _filt_spec_kernel,
        out_shape=(spec, spec, spec),
        grid=(L // tf, DEPTH),
        in_specs=[
            pl.BlockSpec((tf, L), lambda i, l: (i, 0)),
            pl.BlockSpec((tf, L), lambda i, l: (i, 0)),
            pl.BlockSpec((None, L, nw), lambda i, l: (l, 0, 0)),
            pl.BlockSpec((None, L, nw), lambda i, l: (l, 0, 0)),
            pl.BlockSpec((None, 8, nw), lambda i, l: (l, 0, 0)),
        ],
        out_specs=(pl.BlockSpec((None, tf, nw), lambda i, l: (l, i, 0)),) * 3,
        compiler_params=_cparams(2),
        name="hyena_filter_spectrum",
    )(cm, sm, fsum, fdiff, nyq)


def _short_conv_kernel(L, za_ref, zb_ref, zc_ref, w_ref, b_ref, o_ref):
    W = HY_W
    for j, z_ref in enumerate((za_ref, zb_ref, zc_ref)):
        z = z_ref[...]
        w = w_ref[:, j * W:(j + 1) * W]
        row = lax.broadcasted_iota(jnp.int32, z.shape, 0)
        prev = jnp.where(row == 0, 0.0, pltpu.roll(z, 1, 0))
        nxt = jnp.where(row == L - 1, 0.0, pltpu.roll(z, L - 1, 0))
        o_ref[:, j * W:(j + 1) * W] = (b_ref[:, j * W:(j + 1) * W] + prev * w[0:1] + z * w[1:2]
                                       + nxt * w[2:3])


def _short_conv(z, conv_w, conv_b):
    B, L, _ = z.shape
    W = HY_W
    nb = (HY_ORDER + 1)
    cb = Z_HY // W
    return pl.pallas_call(
        functools.partial(_short_conv_kernel, L),
        out_shape=jax.ShapeDtypeStruct((B, L, nb * W), F32),
        grid=(B,),
        in_specs=[pl.BlockSpec((None, L, W), lambda b, j=j: (b, 0, cb + j)) for j in range(nb)]
        + [pl.BlockSpec((3, nb * W), lambda b: (0, 0)),
           pl.BlockSpec((1, nb * W), lambda b: (0, 0))],
        out_specs=pl.BlockSpec((None, L, nb * W), lambda b: (b, 0, 0)),
        compiler_params=_cparams(1),
        name="hyena_short_conv",
    )(z, z, z, conv_w, conv_b.reshape(1, nb * W))


def _conv_fwd_kernel(u_ref, c_ref, s_ref, a_ref, b_ref, d_ref, yre_ref, yim_ref):
    u = u_ref[...]
    ure = _dot(c_ref[...], u)
    uim = -_dot(s_ref[...], u)
    rowu = lax.broadcasted_iota(jnp.int32, u.shape, 0)
    nyq = jnp.sum(jnp.where((rowu & 1) == 0, u, -u), axis=0, keepdims=True)
    row = lax.broadcasted_iota(jnp.int32, ure.shape, 0) + pl.program_id(0) * ure.shape[0]
    uim = jnp.where(row == 0, nyq, uim)
    a = a_ref[...]
    bm = b_ref[...]
    yre_ref[...] = (ure * a - uim * bm).astype(BF16)
    yim_ref[...] = (ure * bm + uim * d_ref[...]).astype(BF16)


def _conv_inv_kernel(L, yre_ref, yim_ref, c_ref, s_ref, u_ref, skip_ref, m_ref, o_ref):
    yre = yre_ref[...]
    yim = yim_ref[...]
    y = (jnp.dot(c_ref[...], yre, preferred_element_type=F32)
         - jnp.dot(s_ref[...], yim, preferred_element_type=F32)) * (1.0 / L)
    row = lax.broadcasted_iota(jnp.int32, y.shape, 0) + pl.program_id(0) * y.shape[0]
    sgn = jnp.where((row & 1) == 0, 1.0, -1.0)
    y = y + (sgn * yim[0:1].astype(F32) - yre[0:1].astype(F32)) * (0.5 / L)
    o_ref[...] = m_ref[...] * (y + u_ref[...] * skip_ref[...])


def _long_conv(zc, u_col, m_col, filt, layer, order, skip, u_arr=None):
    B, L, _ = zc.shape
    W = HY_W
    cm, sm = _dft_mats(L)
    fa, fb, fd = filt
    tf = min(L, 512)
    if u_arr is None:
        u_arr, ub = zc, u_col
    else:
        ub = 0
    spec_t = pl.BlockSpec((tf, L), lambda i, b: (i, 0))
    fspec = pl.BlockSpec((None, tf, W), lambda i, b: (layer, i, order))
    yre, yim = pl.pallas_call(
        _conv_fwd_kernel,
        out_shape=(jax.ShapeDtypeStruct((B, L, W), BF16),) * 2,
        grid=(L // tf, B),
        in_specs=[pl.BlockSpec((None, L, W), lambda i, b: (b, 0, ub)),
                  spec_t, spec_t, fspec, fspec, fspec],
        out_specs=(pl.BlockSpec((None, tf, W), lambda i, b: (b, i, 0)),) * 2,
        compiler_params=_cparams(2),
        name="hyena_conv_fwd",
    )(u_arr, cm, sm, fa, fb, fd)
    full = pl.BlockSpec((None, L, W), lambda i, b: (b, 0, 0))
    return pl.pallas_call(
        functools.partial(_conv_inv_kernel, L),
        out_shape=jax.ShapeDtypeStruct((B, L, W), F32),
        grid=(L // tf, B),
        in_specs=[full, full, spec_t, spec_t,
                  pl.BlockSpec((None, tf, W), lambda i, b: (b, i, ub)),
                  pl.BlockSpec((None, 1, W), lambda i, b: (order, 0, 0)),
                  pl.BlockSpec((None, tf, W), lambda i, b: (b, i, m_col))],
        out_specs=pl.BlockSpec((None, tf, W), lambda i, b: (b, i, 0)),
        compiler_params=_cparams(2),
        name="hyena_conv_inv",
    )(yre, yim, cm, sm, u_arr, skip.reshape(HY_ORDER, 1, W), zc)


def _hyena(z, filt, layer, conv_w, conv_b, skip):
    zc = _short_conv(z, conv_w, conv_b)
    y1 = _long_conv(zc, 0, 1, filt, layer, 0, skip)
    return _long_conv(zc, 0, 2, filt, layer, 1, skip, u_arr=y1)


def _out_kernel(cond_base, cond_stride, hg_ref, da_ref, hy_ref, x_ref, mod_ref, g_ref, w_ref, wr_ref,
                x1_ref, h2_ref, aff_ref):
    row = cond_base + cond_stride * pl.program_id(0)
    D = D_MODEL
    gate1 = mod_ref[pl.ds(row, 1), pl.ds(2 * D, D)]
    shift2 = mod_ref[pl.ds(row, 1), pl.ds(3 * D, D)]
    scale2 = mod_ref[pl.ds(row, 1), pl.ds(4 * D, D)]
    mix = (_dot(hg_ref[...], w_ref[0:HG_W, :])
           + _dot(da_ref[...], w_ref[HG_W:HG_W + DA_W, :])
           + _dot(hy_ref[...], w_ref[HG_W + DA_W:D, :]))
    x1 = x_ref[...] + gate1 * mix
    x1_ref[...] = x1
    h2 = _rms(x1) * g_ref[...] * (1.0 + scale2) + shift2
    h2_ref[...] = h2.astype(BF16)
    wr = wr_ref[...]
    h_hi = h2.astype(BF16)
    w_hi = wr.astype(BF16)
    h_lo = (h2 - h_hi.astype(F32)).astype(BF16)
    w_lo = (wr - w_hi.astype(F32)).astype(BF16)
    logits = (jnp.dot(h_hi, w_hi, preferred_element_type=F32) + jnp.dot(h_hi, w_lo, preferred_element_type=F32)
              + jnp.dot(h_lo, w_hi, preferred_element_type=F32))
    lane = lax.broadcasted_iota(jnp.int32, logits.shape, 1)
    logits = jnp.where(lane < N_EXPERTS, logits, -jnp.inf)
    e = jnp.exp(logits - jnp.max(logits, axis=1, keepdims=True))
    aff_ref[...] = e / jnp.sum(e, axis=1, keepdims=True)


def _out_proj(o_hg, o_da, o_hy, x, mod, g, w_bf, wr_pad, cond_base, cond_stride, layer):
    B, L, D = x.shape
    tm = min(L, 512)

    def rows(w):
        return pl.BlockSpec((None, tm, w), lambda b, i: (b, i, 0))

    return pl.pallas_call(
        functools.partial(_out_kernel, cond_base, cond_stride),
        out_shape=(jax.ShapeDtypeStruct((B, L, D), F32),
                   jax.ShapeDtypeStruct((B, L, D), BF16),
                   jax.ShapeDtypeStruct((B, L, LANES), F32)),
        grid=(B, L // tm),
        in_specs=[rows(HG_W), rows(DA_W), rows(HY_W), rows(D),
                  pl.BlockSpec((None, 8, N_MOD * D), lambda b, i: (layer, 0, 0)),
                  pl.BlockSpec((1, D), lambda b, i: (0, 0)),
                  pl.BlockSpec((None, D, D), lambda b, i: (layer, 0, 0)),
                  pl.BlockSpec((None, D, LANES), lambda b, i: (layer, 0, 0))],
        out_specs=(rows(D), rows(D), rows(LANES)),
        compiler_params=_cparams(2),
        name="out_proj",
    )(o_hg, o_da, o_hy, x, mod, g, w_bf, wr_pad)


def _prefix_count(m, blk):
    L = m.shape[0]
    r = lax.broadcasted_iota(jnp.int32, (blk, blk), 0)
    c = lax.broadcasted_iota(jnp.int32, (blk, blk), 1)
    tri = jnp.where(c < r, 1.0, 0.0).astype(BF16)
    carry = jnp.zeros((1, m.shape[1]), F32)
    parts = []
    for i in range(L // blk):
        mb = m[i * blk:(i + 1) * blk]
        parts.append(jnp.dot(tri, mb.astype(BF16), preferred_element_type=F32) + carry)
        carry = carry + jnp.sum(mb, axis=0, keepdims=True)
    return jnp.concatenate(parts, axis=0) if len(parts) > 1 else parts[0]


def _route_kernel(cap, aff_ref, slot_ref, slot_t_ref):
    aff = aff_ref[...]
    capf = float(cap)
    tiny = 2.0 ** -126

    def enough(t):
        return jnp.sum(jnp.where(aff >= t, 1.0, 0.0), axis=0, keepdims=True) >= capf

    has = enough(jnp.full((1, LANES), tiny, F32))
    lo = jnp.full((1, LANES), tiny, F32)
    for s in (64, 32, 16, 8, 4, 2, 1):
        cand = lo * (2.0 ** s)
        lo = jnp.where(enough(cand), cand, lo)
    step = lo
    for _ in range(23):
        step = step * 0.5
        cand = lo + step
        lo = jnp.where(enough(cand), cand, lo)
    hi = jnp.where(has, lo + jnp.maximum(step, tiny), tiny)
    lo = jnp.where(has, lo, 0.0)
    gt = jnp.where(aff >= hi, 1.0, 0.0)
    eq = jnp.where((aff >= lo) & (aff < hi), 1.0, 0.0)
    need = capf - jnp.sum(gt, axis=0, keepdims=True)
    blk = min(aff.shape[0], 256)
    sel = gt + eq * jnp.where(_prefix_count(eq, blk) < need, 1.0, 0.0)
    lane = lax.broadcasted_iota(jnp.int32, aff.shape, 1)
    sel = jnp.where(lane < N_EXPERTS, sel, 0.0)
    slot = jnp.where(sel > 0.0, _prefix_count(sel, blk), -1.0)
    slot_ref[...] = slot
    slot_t_ref[...] = jnp.transpose(slot)[0:N_EXPERTS, :]


def _route(aff, cap):
    B, L, _ = aff.shape
    return pl.pallas_call(
        functools.partial(_route_kernel, cap),
        out_shape=(jax.ShapeDtypeStruct((B, L, LANES), F32),
                   jax.ShapeDtypeStruct((B, N_EXPERTS, L), F32)),
        grid=(B,),
        in_specs=[pl.BlockSpec((None, L, LANES), lambda b: (b, 0, 0))],
        out_specs=(pl.BlockSpec((None, L, LANES), lambda b: (b, 0, 0)),
                   pl.BlockSpec((None, N_EXPERTS, L), lambda b: (b, 0, 0))),
        compiler_params=_cparams(1),
        name="route",
    )(aff)


def _gather_kernel(cap, eg, slot_t_ref, h_ref, xs_ref):
    e0 = pl.program_id(1) * eg
    c = lax.broadcasted_iota(jnp.int32, (cap, slot_t_ref.shape[1]), 0).astype(F32)
    onehot = jnp.concatenate(
        [jnp.where(c == slot_t_ref[pl.ds(e0 + j, 1), :], 1.0, 0.0) for j in range(eg)],
        axis=0).astype(BF16)
    xs = jnp.dot(onehot, h_ref[...], preferred_element_type=F32).astype(BF16)
    for j in range(eg):
        xs_ref[j] = xs[j * cap:(j + 1) * cap]


def _gather(slot_t, h2, cap):
    B, L, D = h2.shape
    eg = max(1, min(N_EXPERTS, GATHER_ROWS // cap))
    return pl.pallas_call(
        functools.partial(_gather_kernel, cap, eg),
        out_shape=jax.ShapeDtypeStruct((N_EXPERTS, B * cap, D), BF16),
        grid=(B, N_EXPERTS // eg),
        in_specs=[pl.BlockSpec((None, N_EXPERTS, L), lambda b, e: (b, 0, 0)),
                  pl.BlockSpec((None, L, D), lambda b, e: (b, 0, 0))],
        out_specs=pl.BlockSpec((eg, cap, D), lambda b, e: (e, b, 0)),
        compiler_params=_cparams(2),
        name="expert_gather",
    )(slot_t, h2)


def _ffn_kernel(xa_ref, xb_ref, wg_ref, wu_ref, wd_ref, oa_ref, ob_ref):
    wg = wg_ref[...].astype(BF16)
    wu = wu_ref[...].astype(BF16)
    wd = wd_ref[...].astype(BF16)
    for x_ref, o_ref in ((xa_ref, oa_ref), (xb_ref, ob_ref)):
        x = x_ref[...]
        hid = (_silu(jnp.dot(x, wg, preferred_element_type=F32))
               * jnp.dot(x, wu, preferred_element_type=F32))
        o_ref[...] = jnp.dot(hid.astype(BF16), wd, preferred_element_type=F32).astype(BF16)


def _expert_ffn(xs_a, xs_b, w_gate, w_up, w_down, layer):
    E, S, D = xs_a.shape
    xspec = pl.BlockSpec((None, S, D), lambda e: (e, 0, 0))
    out = jax.ShapeDtypeStruct((E, S, D), BF16)
    return pl.pallas_call(
        _ffn_kernel,
        out_shape=(out, out),
        grid=(E,),
        in_specs=[xspec, xspec,
                  pl.BlockSpec((None, None, D, D_FF), lambda e: (layer, e, 0, 0)),
                  pl.BlockSpec((None, None, D, D_FF), lambda e: (layer, e, 0, 0)),
                  pl.BlockSpec((None, None, D_FF, D), lambda e: (layer, e, 0, 0))],
        out_specs=(xspec, xspec),
        compiler_params=_cparams(1),
        name="expert_ffn",
    )(xs_a, xs_b, w_gate, w_up, w_down)


def _scatter_kernel(cap, eg, cond_base, cond_stride, final, ys_ref, slot_ref, aff_ref, x_ref, mod_ref, fg_ref,
                    o_ref):
    row = cond_base + cond_stride * pl.program_id(0)
    gate2 = mod_ref[pl.ds(row, 1), pl.ds(5 * D_MODEL, D_MODEL)]
    slot = slot_ref[...].astype(BF16)
    aff = aff_ref[...].astype(BF16)
    tt = slot.shape[0]
    K = eg * cap
    shift = cap.bit_length() - 1
    col_expert = lax.broadcasted_iota(jnp.int32, (LANES, K), 1) >> shift
    lane_expert = lax.broadcasted_iota(jnp.int32, (LANES, K), 0)
    col_slot = (lax.broadcasted_iota(jnp.int32, (tt, K), 1) & (cap - 1)).astype(F32)
    y = jnp.zeros((tt, D_MODEL), F32)
    for g in range(N_EXPERTS // eg):
        expand = jnp.where(lane_expert == col_expert + g * eg, 1.0, 0.0).astype(BF16)
        slot_x = jnp.dot(slot, expand, preferred_element_type=F32)
        aff_x = jnp.dot(aff, expand, preferred_element_type=F32)
        w = jnp.where(slot_x == col_slot, aff_x, 0.0).astype(BF16)
        ys = jnp.concatenate([ys_ref[g * eg + j] for j in range(eg)], axis=0)
        y = y + jnp.dot(w, ys, preferred_element_type=F32)
    x2 = x_ref[...] + gate2 * y
    if final:
        x2 = _rms(x2) * fg_ref[...]
    o_ref[...] = x2


def _scatter(ys, slot, aff, x1, mod, fg, cap, cond_base, cond_stride, final, layer):
    B, L, D = x1.shape
    tt = min(L, 1024)
    assert cap & (cap - 1) == 0
    eg = max(1, min(N_EXPERTS, GATHER_ROWS // cap))

    def rows(w):
        return pl.BlockSpec((None, tt, w), lambda b, i: (b, i, 0))

    return pl.pallas_call(
        functools.partial(_scatter_kernel, cap, eg, cond_base, cond_stride, final),
        out_shape=jax.ShapeDtypeStruct((B, L, D), F32),
        grid=(B, L // tt),
        in_specs=[pl.BlockSpec((N_EXPERTS, cap, D), lambda b, i: (0, b, 0)),
                  rows(LANES), rows(LANES), rows(D),
                  pl.BlockSpec((None, 8, N_MOD * D), lambda b, i: (layer, 0, 0)),
                  pl.BlockSpec((1, D), lambda b, i: (0, 0))],
        out_specs=rows(D),
        compiler_params=_cparams(2),
        name="expert_scatter",
    )(ys, slot, aff, x1, mod, fg)


def _state_to_kernel_layout(s):
    B = s.shape[0]
    st = jnp.swapaxes(s, -1, -2).reshape(B, 2, 2, 2, HG_DK, HG_DK)
    eye = jnp.eye(2, dtype=s.dtype)
    full = st[:, :, :, :, :, None, :] * eye[None, None, None, :, None, :, None]
    return full.reshape(B, 2, 2, 2 * HG_DK, 2 * HG_DK)


def _state_from_kernel_layout(st):
    B = st.shape[0]
    full = st.reshape(B, 2, 2, 2, HG_DK, 2, HG_DK)
    diag = jnp.stack([full[:, :, :, h, :, h, :] for h in range(2)], axis=3)
    return jnp.swapaxes(diag.reshape(B, 2, HG_HEADS, HG_DK, HG_DK), -1, -2)


def kernel(x_prompt, x_sample, c, cache_diff_k, cache_diff_v, state_hgrn, c_ctx, norm_g, final_norm_g,
           w_mod, b_mod, w_in, w_out, hgrn_lb_logits, hgrn_norm_g, diff_lambda, diff_subln_g,
           hy_conv_w, hy_conv_b, hy_filt_w1, hy_filt_b1, hy_filt_w2, hy_filt_b2, hy_filt_w3, hy_filt_freq,
           hy_skip, w_router, w_gate, w_up, w_down):
    B_p, L_p, D = x_prompt.shape
    B_s, L_s, _ = x_sample.shape
    cap_p = EC_CAPACITY_FACTOR * L_p // N_EXPERTS
    cap_s = EC_CAPACITY_FACTOR * L_s // N_EXPERTS
    assert B_p * cap_p == B_s * cap_s

    cond8 = jnp.zeros((8, D), F32).at[0].set(c_ctx.astype(F32)).at[1:1 + B_s].set(c.astype(F32))
    mod = _modulation(cond8, w_mod, b_mod)
    groups = (
        dict(cond=(0, 0), cap=cap_p, L=L_p),
        dict(cond=(1, 1), cap=cap_s, L=L_s),
    )
    filt = {L: _hyena_filters(L, hy_filt_w1, hy_filt_b1, hy_filt_w2, hy_filt_b2, hy_filt_w3, hy_filt_freq)
            for L in {L_p, L_s}}
    rope_tabs = _rope_tables(L_s)
    fg = final_norm_g.astype(F32).reshape(1, D)
    wr_pad = jnp.pad(w_router.astype(F32), ((0, 0), (0, 0), (0, LANES - N_EXPERTS)))
    s0 = jnp.zeros((B_p, 2, 2, HG_W // 2, HG_W // 2), F32)
    caches = None
    w_in_bf = w_in.astype(BF16)
    w_out_bf = w_out.astype(BF16)

    xs = [x_prompt.astype(F32), x_sample.astype(F32)]
    new_s = []
    for l in range(DEPTH):
        ng = jnp.tile(hgrn_norm_g[l].astype(F32), HG_HEADS).reshape(1, HG_W)
        sg = diff_subln_g[l].astype(F32).reshape(1, DA_DV)
        g1 = norm_g[l, 0].astype(F32).reshape(1, D)
        g2 = norm_g[l, 1].astype(F32).reshape(1, D)
        x1s, slots, affs, gathered = [], [], [], []
        for gi, grp in enumerate(groups):
            x = xs[gi]
            cb, cs = grp["cond"]
            if gi == 0:
                z, qa, ka, va, kc, vc = _in_proj(x, mod, g1, w_in_bf, cb, cs, l, caches=caches,
                                                 cache_dtype=x_prompt.dtype)
                caches = (kc, vc)
                s_init = s0
            else:
                z, qa, ka, va = _in_proj(x, mod, g1, w_in_bf, cb, cs, l, rope_tabs=rope_tabs)
                s_init = _state_to_kernel_layout(state_hgrn[:, l].astype(F32))
            o_hg, s_fin = _hgrn(z, hgrn_lb_logits.astype(F32), s_init, ng, l)
            if gi == 0:
                new_s.append(_state_from_kernel_layout(s_fin))
                o_da = _attention(qa, ka, va, l, diff_lambda[l].astype(F32), sg)
            else:
                o_da = _attention(qa, ka, va, l, diff_lambda[l].astype(F32), sg,
                                  cache_diff_k.astype(F32), cache_diff_v.astype(F32))
            o_hy = _hyena(z, filt[grp["L"]], l, hy_conv_w[l].astype(F32), hy_conv_b[l].astype(F32),
                          hy_skip[l].astype(F32))
            x1, h2, aff = _out_proj(o_hg, o_da, o_hy, x, mod, g2, w_out_bf, wr_pad, cb, cs, l)
            slot, slot_t = _route(aff, grp["cap"])
            x1s.append(x1)
            slots.append(slot)
            affs.append(aff)
            gathered.append(_gather(slot_t, h2, grp["cap"]))
        ys = _expert_ffn(gathered[0], gathered[1], w_gate, w_up, w_down, l)
        for gi, grp in enumerate(groups):
            cb, cs = grp["cond"]
            xs[gi] = _scatter(ys[gi], slots[gi], affs[gi], x1s[gi], mod, fg, grp["cap"], cb, cs,
                              final=(l == DEPTH - 1), layer=l)

    dt = x_prompt.dtype
    return (xs[0].astype(dt), xs[1].astype(x_sample.dtype),
            caches[0], caches[1], jnp.stack(new_s, axis=1).astype(dt))
```

```python
import functools
import math

import jax
import jax.numpy as jnp
import numpy as np
from jax import lax
from jax.experimental import pallas as pl
from jax.experimental.pallas import tpu as pltpu

F32 = jnp.float32
BF16 = jnp.bfloat16
HI = lax.Precision.HIGHEST

D_MODEL = 1024
DEPTH = 4
GRID_W = 64
HG_HEADS = 4
HG_DK = 64
HG_W = 256
DA_HEADS = 4
DA_DQK = 64
DA_DV = 128
DA_W = 512
HY_W = 256
HY_ORDER = 2
HY_POS_EMB = 33
HY_HIDDEN = 64
HY_SHORT_DECAY_PCT = 0.3
HY_LONG_DECAY_PCT = 1.5
HY_TARGET = 1e-2
N_EXPERTS = 16
EC_CAPACITY_FACTOR = 2
D_FF = 1024
N_MOD = 6
ROPE_BASE = 10000.0
NORM_EPS = 1e-6
D_IN = 3584
COL_HQ, COL_HFF, COL_HFB, COL_HI, COL_HG = 0, 256, 512, 768, 1024
COL_AQ, COL_AK, COL_AV, COL_HY = 1280, 1792, 2304, 2816
Z_W = COL_AQ + (D_IN - COL_HY)
Z_HY = COL_AQ

LANES = 128
HGRN_CHUNK = 16
HGRN_GROUP = 8
GATHER_ROWS = 1024
SHORT_SEQ_ROWS = 1024
VMEM_LIMIT = 56 * 1024 * 1024


def _cparams(n_axes):
    return pltpu.CompilerParams(dimension_semantics=("arbitrary",) * n_axes,
                                vmem_limit_bytes=VMEM_LIMIT)


def _requests_per_step(B, L):
    rb = max(1, SHORT_SEQ_ROWS // L)
    while B % rb:
        rb -= 1
    return rb


def _dot(a, b):
    return jnp.dot(a.astype(BF16), b.astype(BF16), preferred_element_type=F32)


def _dot_nt(a, b):
    return lax.dot_general(a.astype(BF16), b.astype(BF16), (((1,), (1,)), ((), ())),
                           preferred_element_type=F32)


def _dot_tn(a, b):
    return lax.dot_general(a.astype(BF16), b.astype(BF16), (((0,), (0,)), ((), ())),
                           preferred_element_type=F32)


def _dot_hi(a, b):
    return jnp.dot(a, b, precision=HI, preferred_element_type=F32)


def _sigmoid(x):
    return 1.0 / (1.0 + jnp.exp(-x))


def _silu(x):
    return x * _sigmoid(x)


def _mod_kernel(cond_ref, w_ref, b_ref, o_ref):
    s = _silu(cond_ref[...])
    o_ref[...] = _dot(s, w_ref[...]) + b_ref[...]


def _modulation(cond8, w_mod, b_mod):
    tn = 1536
    n_mod = N_MOD * D_MODEL
    return pl.pallas_call(
        _mod_kernel,
        out_shape=jax.ShapeDtypeStruct((DEPTH, 8, n_mod), F32),
        grid=(DEPTH, n_mod // tn),
        in_specs=[
            pl.BlockSpec((8, D_MODEL), lambda l, j: (0, 0)),
            pl.BlockSpec((None, D_MODEL, tn), lambda l, j: (l, 0, j)),
            pl.BlockSpec((None, 1, tn), lambda l, j: (l, 0, j)),
        ],
        out_specs=pl.BlockSpec((None, 8, tn), lambda l, j: (l, 0, j)),
        compiler_params=_cparams(2),
        name="modulation",
    )(cond8, w_mod, b_mod.reshape(DEPTH, 1, n_mod))


def _rms(x):
    return x * lax.rsqrt(jnp.mean(x * x, axis=-1, keepdims=True) + NORM_EPS)


def _rope(x, cos, sin):
    lane = lax.broadcasted_iota(jnp.int32, x.shape, 1)
    swapped = jnp.where((lane & 31) < 16, -pltpu.roll(x, LANES - 16, 1), pltpu.roll(x, 16, 1))
    return x * cos + swapped * sin


def _in_kernel(cond_base, cond_stride, latent, x_ref, mod_ref, g_ref, w_ref, *rest):
    if latent:
        cos_ref, sin_ref, z_ref, q_ref, k_ref, v_ref = rest
    else:
        z_ref, q_ref, k_ref, v_ref, kc_ref, vc_ref = rest[-6:]
    row = cond_base + cond_stride * pl.program_id(0)
    shift = mod_ref[pl.ds(row, 1), pl.ds(0, D_MODEL)]
    scale = mod_ref[pl.ds(row, 1), pl.ds(D_MODEL, D_MODEL)]
    h = (_rms(x_ref[...]) * g_ref[...] * (1.0 + scale) + shift).astype(BF16)
    z_ref[:, 0:COL_AQ] = jnp.dot(h, w_ref[:, 0:COL_AQ], preferred_element_type=F32)
    z_ref[:, COL_AQ:Z_W] = jnp.dot(h, w_ref[:, COL_HY:D_IN], preferred_element_type=F32)
    att = jnp.dot(h, w_ref[:, COL_AQ:COL_HY], preferred_element_type=F32)
    q = att[:, 0:DA_W]
    k = att[:, DA_W:2 * DA_W]
    v = att[:, 2 * DA_W:3 * DA_W]
    if latent:
        cos = cos_ref[...]
        sin = sin_ref[...]
        heads = [slice(hh * LANES, (hh + 1) * LANES) for hh in range(DA_HEADS)]
        q = jnp.concatenate([_rope(q[:, s], cos, sin) for s in heads], axis=1)
        k = jnp.concatenate([_rope(k[:, s], cos, sin) for s in heads], axis=1)
    else:
        for hh in range(DA_HEADS):
            for m in range(2):
                c0 = (2 * hh + m) * DA_DQK
                kc_ref[hh, m] = k[:, c0:c0 + DA_DQK].astype(kc_ref.dtype)
            vc_ref[hh] = v[:, hh * DA_DV:(hh + 1) * DA_DV].astype(vc_ref.dtype)
    q_ref[...] = (q * (DA_DQK ** -0.5)).astype(BF16)
    k_ref[...] = k.astype(BF16)
    v_ref[...] = v.astype(BF16)


def _in_proj(x, mod, g, w_bf, cond_base, cond_stride, layer, rope_tabs=None, caches=None, cache_dtype=None):
    B, L, _ = x.shape
    tm = 256
    latent = rope_tabs is not None
    rows = lambda w: pl.BlockSpec((None, tm, w), lambda b, i: (b, i, 0))
    in_specs = [
        rows(D_MODEL),
        pl.BlockSpec((None, 8, N_MOD * D_MODEL), lambda b, i: (layer, 0, 0)),
        pl.BlockSpec((1, D_MODEL), lambda b, i: (0, 0)),
        pl.BlockSpec((None, D_MODEL, D_IN), lambda b, i: (layer, 0, 0)),
    ]
    args = [x, mod, g, w_bf]
    out_shape = [jax.ShapeDtypeStruct((B, L, Z_W), F32)] + [jax.ShapeDtypeStruct((B, L, DA_W), BF16)] * 3
    out_specs = [rows(Z_W), rows(DA_W), rows(DA_W), rows(DA_W)]
    aliases = {}
    if latent:
        in_specs += [pl.BlockSpec((tm, LANES), lambda b, i: (i, 0))] * 2
        args += list(rope_tabs)
    else:
        assert tm == L
        out_shape += [jax.ShapeDtypeStruct((B, DEPTH, DA_HEADS, 2, L, DA_DQK), cache_dtype),
                      jax.ShapeDtypeStruct((B, DEPTH, DA_HEADS, L, DA_DV), cache_dtype)]
        out_specs += [pl.BlockSpec((None, None, DA_HEADS, 2, tm, DA_DQK), lambda b, i: (b, layer, 0, 0, i, 0)),
                      pl.BlockSpec((None, None, DA_HEADS, tm, DA_DV), lambda b, i: (b, layer, 0, i, 0))]
        if caches is not None:
            in_specs += [pl.BlockSpec(memory_space=pl.ANY)] * 2
            args += list(caches)
            aliases = {4: 4, 5: 5}
    return pl.pallas_call(
        functools.partial(_in_kernel, cond_base, cond_stride, latent),
        out_shape=tuple(out_shape),
        grid=(B, L // tm),
        in_specs=in_specs,
        out_specs=tuple(out_specs),
        input_output_aliases=aliases,
        compiler_params=_cparams(2),
        name="in_proj",
    )(*args)


def _head_block_mask(n, width):
    r = lax.broadcasted_iota(jnp.int32, (n, n), 0) // width
    c = lax.broadcasted_iota(jnp.int32, (n, n), 1) // width
    return r == c


def _hgrn_kernel(L, layer, q_ref, ff_ref, fb_ref, v_ref, g_ref, lbl_ref, s0_ref, ng_ref,
                 o_ref, sfin_ref, qh_s, vh_s, bf_s, bb_s, kf_s, kb_s, of_s, ob_s, st_s):
    C = HGRN_CHUNK
    G = HGRN_GROUP
    R = G * C
    W = HG_W
    HALF = W // 2
    ng_groups = L // R

    lg = [lbl_ref[j] for j in range(DEPTH)]
    mx = functools.reduce(jnp.maximum, lg)
    ex = [jnp.exp(a - mx) for a in lg]
    den = functools.reduce(lambda a, b: a + b, ex)
    sm = [e / den for e in ex]
    cum = sm[0]
    for j in range(1, layer + 1):
        cum = cum + sm[j]
    lb = cum - sm[0]

    rowc = lax.broadcasted_iota(jnp.int32, (L, W), 0) & (C - 1)

    def gates(f, lbd):
        e = jnp.exp(-jnp.abs(f))
        r = 1.0 / (1.0 + e)
        log_sig = jnp.minimum(f, 0.0) + jnp.log(r)
        sig_neg = jnp.where(f >= 0.0, e * r, r)
        if layer == 0:
            return sig_neg, log_sig
        a = jnp.log(lbd)
        c = jnp.log1p(-lbd) + log_sig
        logf = jnp.maximum(a, c) + jnp.log1p(jnp.exp(-jnp.abs(a - c)))
        return (1.0 - lbd) * sig_neg, logf

    kf, lf = gates(ff_ref[...], lb[0:1])
    kb, lbk = gates(fb_ref[...], lb[1:2])

    def put(ref, x):
        ref[0] = x[:, 0:HALF]
        ref[1] = x[:, HALF:W]

    def get(ref, rows):
        return jnp.concatenate([ref.at[0][rows, :], ref.at[1][rows, :]], axis=1)

    put(qh_s, q_ref[...])
    put(vh_s, v_ref[...])
    put(kf_s, kf)
    put(kb_s, kb)
    d = 1
    while d < C:
        lf = lf + jnp.where(rowc >= d, pltpu.roll(lf, d, 0), 0.0)
        lbk = lbk + jnp.where(rowc < C - d, pltpu.roll(lbk, L - d, 0), 0.0)
        d *= 2
    put(bf_s, lf)
    put(bb_s, lbk)

    st_s[...] = s0_ref[...]
    bd_bf = jnp.where(_head_block_mask(W, HG_DK), 1.0, 0.0).astype(BF16)
    bd_half = _head_block_mask(HALF, HG_DK)

    def head_sum(p):
        return jnp.dot(p.astype(BF16), bd_bf, preferred_element_type=F32)

    def group(r0, b_s, k_s, o_s, d, forward):
        def pos(ref, s):
            return get(ref, pl.ds(r0 + s, G, stride=C))

        pairs = [(t, s) for s in range(C) for t in (range(s, C) if forward else range(s + 1))]
        prods = [pos(qh_s, t) * pos(k_s, s) if t == s
                 else jnp.exp(pos(b_s, t) - pos(b_s, s)) * pos(qh_s, t) * pos(k_s, s) for t, s in pairs]
        a = head_sum(jnp.concatenate(prods, axis=0))
        acc = [None] * C
        for i, (t, s) in enumerate(pairs):
            term = a[i * G:(i + 1) * G] * pos(vh_s, s)
            acc[t] = term if acc[t] is None else acc[t] + term
        for t in range(C):
            o_s.at[0][pl.ds(r0 + t, G, stride=C), :] = acc[t][:, 0:HALF]
            o_s.at[1][pl.ds(r0 + t, G, stride=C), :] = acc[t][:, HALF:W]
        order = list(range(G)) if forward else list(range(G - 1, -1, -1))
        halves = [slice(p * HALF, (p + 1) * HALF) for p in range(2)]
        qe, kv, dec = {}, {}, {}
        for c in order:
            rows = pl.ds(r0 + c * C, C)
            b = get(b_s, rows)
            v = get(vh_s, rows)
            b_end = b[C - 1:C] if forward else b[0:1]
            qe[c] = get(qh_s, rows) * jnp.exp(b)
            ke = get(k_s, rows) * jnp.exp(b_end - b)
            dec[c] = jnp.exp(b_end)
            kv[c] = [jnp.where(bd_half, _dot_tn(v[:, ln], ke[:, ln]), 0.0) for ln in halves]
        st = [st_s[d, p] for p in range(2)]
        before = {}
        for c in order:
            before[c] = [s.astype(BF16) for s in st]
            st = [st[p] * dec[c][:, halves[p]] + kv[c][p] for p in range(2)]
        for p in range(2):
            st_s[d, p] = st[p]
        for c in order:
            rows = pl.ds(r0 + c * C, C)
            for p in range(2):
                o_s.at[p][rows, :] = o_s.at[p][rows, :] + _dot_nt(qe[c][:, halves[p]], before[c][p])

    def body(i, carry):
        group(pl.multiple_of(i * R, R), bf_s, kf_s, of_s, 0, True)
        group(pl.multiple_of((ng_groups - 1 - i) * R, R), bb_s, kb_s, ob_s, 1, False)
        return carry

    lax.fori_loop(0, ng_groups, body, 0)

    o = jnp.concatenate([of_s[0] + ob_s[0], of_s[1] + ob_s[1]], axis=1)
    ms = _dot_hi(o * o, jnp.where(_head_block_mask(W, HG_DK), 1.0 / HG_DK, 0.0).astype(F32))
    o_ref[...] = o * lax.rsqrt(ms + NORM_EPS) * ng_ref[...] * _silu(g_ref[...])
    sfin_ref[...] = st_s[...]


def _hgrn(z, lb_logits, s0t, ng_tiled, layer):
    B, L, _ = z.shape
    W = HG_W
    HALF = W // 2

    def col(c):
        return pl.BlockSpec((None, L, W), lambda b, c=c: (b, 0, c // W))

    seq = pltpu.VMEM((2, L, HALF), F32)
    st_spec = pl.BlockSpec((None, 2, 2, HALF, HALF), lambda b: (b, 0, 0, 0, 0))
    return pl.pallas_call(
        functools.partial(_hgrn_kernel, L, layer),
        out_shape=(jax.ShapeDtypeStruct((B, L, W), F32),
                   jax.ShapeDtypeStruct((B, 2, 2, HALF, HALF), F32)),
        grid=(B,),
        in_specs=[col(COL_HQ), col(COL_HFF), col(COL_HFB), col(COL_HI), col(COL_HG),
                  pl.BlockSpec((DEPTH, 2, W), lambda b: (0, 0, 0)),
                  st_spec,
                  pl.BlockSpec((1, W), lambda b: (0, 0))],
        out_specs=(pl.BlockSpec((None, L, W), lambda b: (b, 0, 0)), st_spec),
        scratch_shapes=[seq] * 8 + [pltpu.VMEM((2, 2, HALF, HALF), F32)],
        compiler_params=_cparams(1),
        name="hgrn",
    )(z, z, z, z, z, lb_logits, s0t, ng_tiled)


def _attn_kernel(latent, lam_init, n_heads, q_ref, k_ref, v_ref, *rest):
    if latent:
        ck_ref, cv_ref, lam_ref, sg_ref, o_ref = rest
    else:
        lam_ref, sg_ref, o_ref = rest
    lv = lam_ref[...]
    lam = (jnp.exp(jnp.sum(lv[0:1] * lv[1:2], axis=1, keepdims=True))
           - jnp.exp(jnp.sum(lv[2:3] * lv[3:4], axis=1, keepdims=True)) + lam_init)
    for hh in range(n_heads):
        hs = slice(hh * LANES, (hh + 1) * LANES)
        q = q_ref[:, hs]
        k = k_ref[:, hs]
        v = v_ref[:, hs]
        lane = lax.broadcasted_iota(jnp.int32, q.shape, 1)
        outs = []
        for m in range(2):
            qm = jnp.where((lane >= m * DA_DQK) & (lane < (m + 1) * DA_DQK), q, jnp.zeros_like(q))
            s = lax.dot_general(qm, k, (((1,), (1,)), ((), ())), preferred_element_type=F32)
            mx = jnp.max(s, axis=1, keepdims=True)
            if latent:
                sc = _dot_nt(q[:, m * DA_DQK:(m + 1) * DA_DQK], ck_ref[hh, m])
                mx = jnp.maximum(mx, jnp.max(sc, axis=1, keepdims=True))
            p = jnp.exp(s - mx)
            den = jnp.sum(p, axis=1, keepdims=True)
            o = jnp.dot(p.astype(BF16), v, preferred_element_type=F32)
            if latent:
                pc = jnp.exp(sc - mx)
                den = den + jnp.sum(pc, axis=1, keepdims=True)
                o = o + _dot(pc, cv_ref[hh])
            outs.append(o / den)
        o = outs[0] - lam * outs[1]
        o_ref[:, hs] = _rms(o) * sg_ref[...] * (1.0 - lam_init)


def _attention(q, k, v, layer, lam_p, subln_g, ctx_k=None, ctx_v=None):
    B, L, _ = q.shape
    latent = ctx_k is not None
    lam_init = 0.8 - 0.6 * math.exp(-0.3 * layer)
    if latent:
        tq, nh = 256, 4
        grid = (B, DA_HEADS // nh, L // tq)
        P = ctx_k.shape[4]
        in_specs = [
            pl.BlockSpec((None, tq, nh * LANES), lambda b, h, i: (b, i, h)),
            pl.BlockSpec((None, L, nh * LANES), lambda b, h, i: (b, 0, h)),
            pl.BlockSpec((None, L, nh * LANES), lambda b, h, i: (b, 0, h)),
            pl.BlockSpec((None, None, nh, 2, P, DA_DQK), lambda b, h, i: (b, layer, h, 0, 0, 0)),
            pl.BlockSpec((None, None, nh, P, DA_DV), lambda b, h, i: (b, layer, h, 0, 0)),
            pl.BlockSpec((4, DA_DQK), lambda b, h, i: (0, 0)),
            pl.BlockSpec((1, DA_DV), lambda b, h, i: (0, 0)),
        ]
        args = [q, k, v, ctx_k, ctx_v, lam_p, subln_g]
        out_spec = pl.BlockSpec((None, tq, nh * DA_DV), lambda b, h, i: (b, i, h))
    else:
        nh = DA_HEADS
        grid = (B,)
        full = pl.BlockSpec((None, L, DA_W), lambda b: (b, 0, 0))
        in_specs = [full, full, full,
                    pl.BlockSpec((4, DA_DQK), lambda b: (0, 0)),
                    pl.BlockSpec((1, DA_DV), lambda b: (0, 0))]
        args = [q, k, v, lam_p, subln_g]
        out_spec = full
    return pl.pallas_call(
        functools.partial(_attn_kernel, latent, lam_init, nh),
        out_shape=jax.ShapeDtypeStruct((B, L, DA_W), F32),
        grid=grid,
        in_specs=in_specs,
        out_specs=out_spec,
        compiler_params=_cparams(len(grid)),
        name="diff_attention",
    )(*args)


def _rope_tables(L):
    half = DA_DQK // 2
    inv = ROPE_BASE ** (-jnp.arange(0, half, 2, dtype=F32) / half)
    t = jnp.arange(L)
    rows = (t // GRID_W).astype(F32)[:, None] * inv[None, :]
    cols = (t % GRID_W).astype(F32)[:, None] * inv[None, :]
    ang = jnp.concatenate([rows, rows, cols, cols], axis=-1)
    ang = jnp.concatenate([ang, ang], axis=-1)
    return jnp.cos(ang), jnp.sin(ang)


@functools.lru_cache(maxsize=None)
def _dft_np(L):
    idx = (np.arange(L, dtype=np.int64)[:, None] * np.arange(L, dtype=np.int64)[None, :]) % (2 * L)
    ang = idx.astype(np.float64) * (math.pi / L)
    return np.cos(ang).astype(np.float32), np.sin(ang).astype(np.float32)


def _dft_mats(L):
    c, s = _dft_np(L)
    return jnp.asarray(c).astype(BF16), jnp.asarray(s).astype(BF16)


def _filt_hidden_kernel(z_ref, w1_ref, b1_ref, w2_ref, b2_ref, fr_ref, hid_ref):
    fr = fr_ref[...]
    hid = jnp.sin(fr[0:1] * (_dot_hi(z_ref[...], w1_ref[...]) + b1_ref[...]))
    hid_ref[...] = jnp.sin(fr[1:2] * (_dot_hi(hid, w2_ref[...]) + b2_ref[...]))


def _filt_taps_kernel(z_ref, hid_ref, w3f_ref, w3b_ref, dl_ref, sum_ref, diff_ref, nyq_ref):
    hid = hid_ref[...]
    decay = jnp.exp(-z_ref[:, 0:1] * jnp.abs(dl_ref[...]))
    hf = _dot_hi(hid, w3f_ref[...]) * decay
    hb = _dot_hi(hid, w3b_ref[...]) * decay
    nrm = (jnp.sum(jnp.abs(hf), axis=0, keepdims=True)
           + jnp.sum(jnp.abs(hb), axis=0, keepdims=True))
    row = lax.broadcasted_iota(jnp.int32, hf.shape, 0)
    hf = hf / nrm
    hb = jnp.where(row == 0, 0.0, hb / nrm)
    fs = hf + hb
    sum_ref[...] = fs
    diff_ref[...] = hf - hb
    sgn = jnp.where((row & 1) == 0, 1.0, -1.0)
    nyq_ref[...] = jnp.broadcast_to(jnp.sum(fs * sgn, axis=0, keepdims=True), nyq_ref.shape)


def _filt_spec_kernel(c_ref, s_ref, sum_ref, diff_ref, nyq_ref, a_ref, b_ref, d_ref):
    a = _dot(c_ref[...], sum_ref[...])
    him = -_dot(s_ref[...], diff_ref[...])
    row = lax.broadcasted_iota(jnp.int32, a.shape, 0) + pl.program_id(0) * a.shape[0]
    a_ref[...] = a
    b_ref[...] = jnp.where(row == 0, 0.0, him)
    d_ref[...] = jnp.where(row == 0, nyq_ref[0:1], a)


def _hyena_filters(L, w1, b1, w2, b2, w3, freq):
    t = jnp.linspace(0.0, 1.0, L, dtype=F32)[:, None]
    bands = (HY_POS_EMB - 1) // 2
    fr = jnp.linspace(1e-4, bands - 1, bands, dtype=F32)[None, :]
    w = 2.0 * math.pi * jnp.arange(L, dtype=F32)[:, None] / L
    z = jnp.concatenate([t, jnp.cos(fr * w), -jnp.sin(fr * w)], axis=-1)
    z = jnp.pad(z, ((0, 0), (0, HY_HIDDEN - HY_POS_EMB)))
    w1p = jnp.pad(w1, ((0, 0), (0, HY_HIDDEN - HY_POS_EMB), (0, 0)))
    max_decay = math.log(HY_TARGET) / HY_SHORT_DECAY_PCT
    min_decay = math.log(HY_TARGET) / HY_LONG_DECAY_PCT
    deltas = jnp.linspace(min_decay, max_decay, HY_W, dtype=F32)[None, :]
    tc = LANES
    nct = HY_W // tc
    nw = HY_ORDER * HY_W
    hid = pl.pallas_call(
        _filt_hidden_kernel,
        out_shape=jax.ShapeDtypeStruct((DEPTH, L, HY_HIDDEN), F32),
        grid=(DEPTH,),
        in_specs=[
            pl.BlockSpec((L, HY_HIDDEN), lambda l: (0, 0)),
            pl.BlockSpec((None, HY_HIDDEN, HY_HIDDEN), lambda l: (l, 0, 0)),
            pl.BlockSpec((None, 1, HY_HIDDEN), lambda l: (l, 0, 0)),
            pl.BlockSpec((None, HY_HIDDEN, HY_HIDDEN), lambda l: (l, 0, 0)),
            pl.BlockSpec((None, 1, HY_HIDDEN), lambda l: (l, 0, 0)),
            pl.BlockSpec((None, 2, HY_HIDDEN), lambda l: (l, 0, 0)),
        ],
        out_specs=pl.BlockSpec((None, L, HY_HIDDEN), lambda l: (l, 0, 0)),
        compiler_params=_cparams(1),
        name="hyena_filter_hidden",
    )(z, w1p, b1.reshape(DEPTH, 1, HY_HIDDEN), w2, b2.reshape(DEPTH, 1, HY_HIDDEN), freq)
    fsum, fdiff, nyq = pl.pallas_call(
        _filt_taps_kernel,
        out_shape=(jax.ShapeDtypeStruct((DEPTH, L, nw), F32),
                   jax.ShapeDtypeStruct((DEPTH, L, nw), F32),
                   jax.ShapeDtypeStruct((DEPTH, 8, nw), F32)),
        grid=(DEPTH, HY_ORDER, nct),
        in_specs=[
            pl.BlockSpec((L, HY_HIDDEN), lambda l, o, j: (0, 0)),
            pl.BlockSpec((None, L, HY_HIDDEN), lambda l, o, j: (l, 0, 0)),
            pl.BlockSpec((None, HY_HIDDEN, tc), lambda l, o, j: (l, 0, o * nct + j)),
            pl.BlockSpec((None, HY_HIDDEN, tc), lambda l, o, j: (l, 0, (HY_ORDER + o) * nct + j)),
            pl.BlockSpec((1, tc), lambda l, o, j: (0, j)),
        ],
        out_specs=(pl.BlockSpec((None, L, tc), lambda l, o, j: (l, 0, o * nct + j)),
                   pl.BlockSpec((None, L, tc), lambda l, o, j: (l, 0, o * nct + j)),
                   pl.BlockSpec((None, 8, tc), lambda l, o, j: (l, 0, o * nct + j))),
        compiler_params=_cparams(3),
        name="hyena_filter_taps",
    )(z, hid, w3, w3, deltas)

    cm, sm = _dft_mats(L)
    tf = min(L, 512)
    spec = jax.ShapeDtypeStruct((DEPTH, L, nw), F32)
    return pl.pallas_call(
        _filt_spec_kernel,
        out_shape=(spec, spec, spec),
        grid=(L // tf, DEPTH),
        in_specs=[
            pl.BlockSpec((tf, L), lambda i, l: (i, 0)),
            pl.BlockSpec((tf, L), lambda i, l: (i, 0)),
            pl.BlockSpec((None, L, nw), lambda i, l: (l, 0, 0)),
            pl.BlockSpec((None, L, nw), lambda i, l: (l, 0, 0)),
            pl.BlockSpec((None, 8, nw), lambda i, l: (l, 0, 0)),
        ],
        out_specs=(pl.BlockSpec((None, tf, nw), lambda i, l: (l, i, 0)),) * 3,
        compiler_params=_cparams(2),
        name="hyena_filter_spectrum",
    )(cm, sm, fsum, fdiff, nyq)


def _short_conv_kernel(L, za_ref, zb_ref, zc_ref, w_ref, b_ref, o_ref):
    W = HY_W
    for r in range(za_ref.shape[0]):
        for j, z_ref in enumerate((za_ref, zb_ref, zc_ref)):
            z = z_ref[r]
            w = w_ref[:, j * W:(j + 1) * W]
            row = lax.broadcasted_iota(jnp.int32, z.shape, 0)
            prev = jnp.where(row == 0, 0.0, pltpu.roll(z, 1, 0))
            nxt = jnp.where(row == L - 1, 0.0, pltpu.roll(z, L - 1, 0))
            o_ref[r, :, j * W:(j + 1) * W] = (b_ref[:, j * W:(j + 1) * W] + prev * w[0:1] + z * w[1:2]
                                              + nxt * w[2:3])


def _short_conv(z, conv_w, conv_b):
    B, L, _ = z.shape
    W = HY_W
    nb = (HY_ORDER + 1)
    cb = Z_HY // W
    rb = _requests_per_step(B, L)
    return pl.pallas_call(
        functools.partial(_short_conv_kernel, L),
        out_shape=jax.ShapeDtypeStruct((B, L, nb * W), F32),
        grid=(B // rb,),
        in_specs=[pl.BlockSpec((rb, L, W), lambda b, j=j: (b, 0, cb + j)) for j in range(nb)]
        + [pl.BlockSpec((3, nb * W), lambda b: (0, 0)),
           pl.BlockSpec((1, nb * W), lambda b: (0, 0))],
        out_specs=pl.BlockSpec((rb, L, nb * W), lambda b: (b, 0, 0)),
        compiler_params=_cparams(1),
        name="hyena_short_conv",
    )(z, z, z, conv_w, conv_b.reshape(1, nb * W))


def _conv_fwd_kernel(u_ref, c_ref, s_ref, a_ref, b_ref, d_ref, yre_ref, yim_ref):
    a = a_ref[...]
    bm = b_ref[...]
    dm = d_ref[...]
    cm = c_ref[...]
    sm = s_ref[...]
    for r in range(u_ref.shape[0]):
        u = u_ref[r]
        ure = _dot(cm, u)
        uim = -_dot(sm, u)
        rowu = lax.broadcasted_iota(jnp.int32, u.shape, 0)
        nyq = jnp.sum(jnp.where((rowu & 1) == 0, u, -u), axis=0, keepdims=True)
        row = lax.broadcasted_iota(jnp.int32, ure.shape, 0) + pl.program_id(0) * ure.shape[0]
        uim = jnp.where(row == 0, nyq, uim)
        yre_ref[r] = (ure * a - uim * bm).astype(BF16)
        yim_ref[r] = (ure * bm + uim * dm).astype(BF16)


def _conv_inv_kernel(L, yre_ref, yim_ref, c_ref, s_ref, u_ref, skip_ref, m_ref, o_ref):
    cm = c_ref[...]
    sm = s_ref[...]
    for r in range(u_ref.shape[0]):
        yre = yre_ref[r]
        yim = yim_ref[r]
        y = (jnp.dot(cm, yre, preferred_element_type=F32)
             - jnp.dot(sm, yim, preferred_element_type=F32)) * (1.0 / L)
        row = lax.broadcasted_iota(jnp.int32, y.shape, 0) + pl.program_id(0) * y.shape[0]
        sgn = jnp.where((row & 1) == 0, 1.0, -1.0)
        y = y + (sgn * yim[0:1].astype(F32) - yre[0:1].astype(F32)) * (0.5 / L)
        o_ref[r] = m_ref[r] * (y + u_ref[r] * skip_ref[...])


def _long_conv(zc, u_col, m_col, filt, layer, order, skip, u_arr=None):
    B, L, _ = zc.shape
    W = HY_W
    cm, sm = _dft_mats(L)
    fa, fb, fd = filt
    tf = min(L, 512)
    rb = _requests_per_step(B, L)
    if u_arr is None:
        u_arr, ub = zc, u_col
    else:
        ub = 0
    spec_t = pl.BlockSpec((tf, L), lambda i, b: (i, 0))
    fspec = pl.BlockSpec((None, tf, W), lambda i, b: (layer, i, order))
    yre, yim = pl.pallas_call(
        _conv_fwd_kernel,
        out_shape=(jax.ShapeDtypeStruct((B, L, W), BF16),) * 2,
        grid=(L // tf, B // rb),
        in_specs=[pl.BlockSpec((rb, L, W), lambda i, b: (b, 0, ub)),
                  spec_t, spec_t, fspec, fspec, fspec],
        out_specs=(pl.BlockSpec((rb, tf, W), lambda i, b: (b, i, 0)),) * 2,
        compiler_params=_cparams(2),
        name="hyena_conv_fwd",
    )(u_arr, cm, sm, fa, fb, fd)
    full = pl.BlockSpec((rb, L, W), lambda i, b: (b, 0, 0))
    return pl.pallas_call(
        functools.partial(_conv_inv_kernel, L),
        out_shape=jax.ShapeDtypeStruct((B, L, W), F32),
        grid=(L // tf, B // rb),
        in_specs=[full, full, spec_t, spec_t,
                  pl.BlockSpec((rb, tf, W), lambda i, b: (b, i, ub)),
                  pl.BlockSpec((None, 1, W), lambda i, b: (order, 0, 0)),
                  pl.BlockSpec((rb, tf, W), lambda i, b: (b, i, m_col))],
        out_specs=pl.BlockSpec((rb, tf, W), lambda i, b: (b, i, 0)),
        compiler_params=_cparams(2),
        name="hyena_conv_inv",
    )(yre, yim, cm, sm, u_arr, skip.reshape(HY_ORDER, 1, W), zc)


def _hyena(z, filt, layer, conv_w, conv_b, skip):
    zc = _short_conv(z, conv_w, conv_b)
    y1 = _long_conv(zc, 0, 1, filt, layer, 0, skip)
    return _long_conv(zc, 0, 2, filt, layer, 1, skip, u_arr=y1)


def _out_kernel(cond_base, cond_stride, hg_ref, da_ref, hy_ref, x_ref, mod_ref, g_ref, w_ref, wr_ref,
                x1_ref, h2_ref, aff_ref):
    row = cond_base + cond_stride * pl.program_id(0)
    D = D_MODEL
    gate1 = mod_ref[pl.ds(row, 1), pl.ds(2 * D, D)]
    shift2 = mod_ref[pl.ds(row, 1), pl.ds(3 * D, D)]
    scale2 = mod_ref[pl.ds(row, 1), pl.ds(4 * D, D)]
    mix = (_dot(hg_ref[...], w_ref[0:HG_W, :])
           + _dot(da_ref[...], w_ref[HG_W:HG_W + DA_W, :])
           + _dot(hy_ref[...], w_ref[HG_W + DA_W:D, :]))
    x1 = x_ref[...] + gate1 * mix
    x1_ref[...] = x1
    h2 = _rms(x1) * g_ref[...] * (1.0 + scale2) + shift2
    h2_ref[...] = h2.astype(BF16)
    wr = wr_ref[...]
    h_hi = h2.astype(BF16)
    w_hi = wr.astype(BF16)
    h_lo = (h2 - h_hi.astype(F32)).astype(BF16)
    w_lo = (wr - w_hi.astype(F32)).astype(BF16)
    logits = (jnp.dot(h_hi, w_hi, preferred_element_type=F32) + jnp.dot(h_hi, w_lo, preferred_element_type=F32)
              + jnp.dot(h_lo, w_hi, preferred_element_type=F32))
    lane = lax.broadcasted_iota(jnp.int32, logits.shape, 1)
    logits = jnp.where(lane < N_EXPERTS, logits, -jnp.inf)
    e = jnp.exp(logits - jnp.max(logits, axis=1, keepdims=True))
    aff_ref[...] = e / jnp.sum(e, axis=1, keepdims=True)


def _out_proj(o_hg, o_da, o_hy, x, mod, g, w_bf, wr_pad, cond_base, cond_stride, layer):
    B, L, D = x.shape
    tm = min(L, 512)

    def rows(w):
        return pl.BlockSpec((None, tm, w), lambda b, i: (b, i, 0))

    return pl.pallas_call(
        functools.partial(_out_kernel, cond_base, cond_stride),
        out_shape=(jax.ShapeDtypeStruct((B, L, D), F32),
                   jax.ShapeDtypeStruct((B, L, D), BF16),
                   jax.ShapeDtypeStruct((B, L, LANES), F32)),
        grid=(B, L // tm),
        in_specs=[rows(HG_W), rows(DA_W), rows(HY_W), rows(D),
                  pl.BlockSpec((None, 8, N_MOD * D), lambda b, i: (layer, 0, 0)),
                  pl.BlockSpec((1, D), lambda b, i: (0, 0)),
                  pl.BlockSpec((None, D, D), lambda b, i: (layer, 0, 0)),
                  pl.BlockSpec((None, D, LANES), lambda b, i: (layer, 0, 0))],
        out_specs=(rows(D), rows(D), rows(LANES)),
        compiler_params=_cparams(2),
        name="out_proj",
    )(o_hg, o_da, o_hy, x, mod, g, w_bf, wr_pad)


def _prefix_count(m, blk):
    L = m.shape[0]
    r = lax.broadcasted_iota(jnp.int32, (blk, blk), 0)
    c = lax.broadcasted_iota(jnp.int32, (blk, blk), 1)
    tri = jnp.where(c < r, 1.0, 0.0).astype(BF16)
    carry = jnp.zeros((1, m.shape[1]), F32)
    parts = []
    for i in range(L // blk):
        mb = m[i * blk:(i + 1) * blk]
        parts.append(jnp.dot(tri, mb.astype(BF16), preferred_element_type=F32) + carry)
        carry = carry + jnp.sum(mb, axis=0, keepdims=True)
    return jnp.concatenate(parts, axis=0) if len(parts) > 1 else parts[0]


def _route_kernel(cap, aff_ref, slot_ref, slot_t_ref):
    for r in range(aff_ref.shape[0]):
        _route_one(cap, aff_ref.at[r], slot_ref.at[r], slot_t_ref.at[r])


def _route_one(cap, aff_ref, slot_ref, slot_t_ref):
    aff = aff_ref[...]
    capf = float(cap)
    tiny = 2.0 ** -126

    def enough(t):
        return jnp.sum(jnp.where(aff >= t, 1.0, 0.0), axis=0, keepdims=True) >= capf

    has = enough(jnp.full((1, LANES), tiny, F32))
    lo = jnp.full((1, LANES), tiny, F32)
    for s in (64, 32, 16, 8, 4, 2, 1):
        cand = lo * (2.0 ** s)
        lo = jnp.where(enough(cand), cand, lo)
    step = lo
    for _ in range(23):
        step = step * 0.5
        cand = lo + step
        lo = jnp.where(enough(cand), cand, lo)
    hi = jnp.where(has, lo + jnp.maximum(step, tiny), tiny)
    lo = jnp.where(has, lo, 0.0)
    gt = jnp.where(aff >= hi, 1.0, 0.0)
    eq = jnp.where((aff >= lo) & (aff < hi), 1.0, 0.0)
    need = capf - jnp.sum(gt, axis=0, keepdims=True)
    blk = min(aff.shape[0], 256)
    sel = gt + eq * jnp.where(_prefix_count(eq, blk) < need, 1.0, 0.0)
    lane = lax.broadcasted_iota(jnp.int32, aff.shape, 1)
    sel = jnp.where(lane < N_EXPERTS, sel, 0.0)
    slot = jnp.where(sel > 0.0, _prefix_count(sel, blk), -1.0)
    slot_ref[...] = slot
    slot_t_ref[...] = jnp.transpose(slot)[0:N_EXPERTS, :]


def _route(aff, cap):
    B, L, _ = aff.shape
    rb = _requests_per_step(B, L)
    return pl.pallas_call(
        functools.partial(_route_kernel, cap),
        out_shape=(jax.ShapeDtypeStruct((B, L, LANES), F32),
                   jax.ShapeDtypeStruct((B, N_EXPERTS, L), F32)),
        grid=(B // rb,),
        in_specs=[pl.BlockSpec((rb, L, LANES), lambda b: (b, 0, 0))],
        out_specs=(pl.BlockSpec((rb, L, LANES), lambda b: (b, 0, 0)),
                   pl.BlockSpec((rb, N_EXPERTS, L), lambda b: (b, 0, 0))),
        compiler_params=_cparams(1),
        name="route",
    )(aff)


def _gather_kernel(cap, eg, slot_t_ref, h_ref, xs_ref):
    e0 = pl.program_id(1) * eg
    c = lax.broadcasted_iota(jnp.int32, (cap, slot_t_ref.shape[1]), 0).astype(F32)
    onehot = jnp.concatenate(
        [jnp.where(c == slot_t_ref[pl.ds(e0 + j, 1), :], 1.0, 0.0) for j in range(eg)],
        axis=0).astype(BF16)
    xs = jnp.dot(onehot, h_ref[...], preferred_element_type=F32).astype(BF16)
    for j in range(eg):
        xs_ref[j] = xs[j * cap:(j + 1) * cap]


def _gather(slot_t, h2, cap):
    B, L, D = h2.shape
    eg = max(1, min(N_EXPERTS, GATHER_ROWS // cap))
    return pl.pallas_call(
        functools.partial(_gather_kernel, cap, eg),
        out_shape=jax.ShapeDtypeStruct((N_EXPERTS, B * cap, D), BF16),
        grid=(B, N_EXPERTS // eg),
        in_specs=[pl.BlockSpec((None, N_EXPERTS, L), lambda b, e: (b, 0, 0)),
                  pl.BlockSpec((None, L, D), lambda b, e: (b, 0, 0))],
        out_specs=pl.BlockSpec((eg, cap, D), lambda b, e: (e, b, 0)),
        compiler_params=_cparams(2),
        name="expert_gather",
    )(slot_t, h2)


def _ffn_kernel(xa_ref, xb_ref, wg_ref, wu_ref, wd_ref, oa_ref, ob_ref):
    wg = wg_ref[...].astype(BF16)
    wu = wu_ref[...].astype(BF16)
    wd = wd_ref[...].astype(BF16)
    for x_ref, o_ref in ((xa_ref, oa_ref), (xb_ref, ob_ref)):
        x = x_ref[...]
        hid = (_silu(jnp.dot(x, wg, preferred_element_type=F32))
               * jnp.dot(x, wu, preferred_element_type=F32))
        o_ref[...] = jnp.dot(hid.astype(BF16), wd, preferred_element_type=F32).astype(BF16)


def _expert_ffn(xs_a, xs_b, w_gate, w_up, w_down, layer):
    E, S, D = xs_a.shape
    xspec = pl.BlockSpec((None, S, D), lambda e: (e, 0, 0))
    out = jax.ShapeDtypeStruct((E, S, D), BF16)
    return pl.pallas_call(
        _ffn_kernel,
        out_shape=(out, out),
        grid=(E,),
        in_specs=[xspec, xspec,
                  pl.BlockSpec((None, None, D, D_FF), lambda e: (layer, e, 0, 0)),
                  pl.BlockSpec((None, None, D, D_FF), lambda e: (layer, e, 0, 0)),
                  pl.BlockSpec((None, None, D_FF, D), lambda e: (layer, e, 0, 0))],
        out_specs=(xspec, xspec),
        compiler_params=_cparams(1),
        name="expert_ffn",
    )(xs_a, xs_b, w_gate, w_up, w_down)


def _scatter_kernel(cap, eg, cond_base, cond_stride, final, ys_ref, slot_ref, aff_ref, x_ref, mod_ref, fg_ref,
                    o_ref):
    row = cond_base + cond_stride * pl.program_id(0)
    gate2 = mod_ref[pl.ds(row, 1), pl.ds(5 * D_MODEL, D_MODEL)]
    slot = slot_ref[...].astype(BF16)
    aff = aff_ref[...].astype(BF16)
    tt = slot.shape[0]
    K = eg * cap
    shift = cap.bit_length() - 1
    col_expert = lax.broadcasted_iota(jnp.int32, (LANES, K), 1) >> shift
    lane_expert = lax.broadcasted_iota(jnp.int32, (LANES, K), 0)
    col_slot = (lax.broadcasted_iota(jnp.int32, (tt, K), 1) & (cap - 1)).astype(F32)
    y = jnp.zeros((tt, D_MODEL), F32)
    for g in range(N_EXPERTS // eg):
        expand = jnp.where(lane_expert == col_expert + g * eg, 1.0, 0.0).astype(BF16)
        slot_x = jnp.dot(slot, expand, preferred_element_type=F32)
        aff_x = jnp.dot(aff, expand, preferred_element_type=F32)
        w = jnp.where(slot_x == col_slot, aff_x, 0.0).astype(BF16)
        ys = jnp.concatenate([ys_ref[g * eg + j] for j in range(eg)], axis=0)
        y = y + jnp.dot(w, ys, preferred_element_type=F32)
    x2 = x_ref[...] + gate2 * y
    if final:
        x2 = _rms(x2) * fg_ref[...]
    o_ref[...] = x2


def _scatter(ys, slot, aff, x1, mod, fg, cap, cond_base, cond_stride, final, layer):
    B, L, D = x1.shape
    tt = min(L, 1024)
    assert cap & (cap - 1) == 0
    eg = max(1, min(N_EXPERTS, GATHER_ROWS // cap))

    def rows(w):
        return pl.BlockSpec((None, tt, w), lambda b, i: (b, i, 0))

    return pl.pallas_call(
        functools.partial(_scatter_kernel, cap, eg, cond_base, cond_stride, final),
        out_shape=jax.ShapeDtypeStruct((B, L, D), F32),
        grid=(B, L // tt),
        in_specs=[pl.BlockSpec((N_EXPERTS, cap, D), lambda b, i: (0, b, 0)),
                  rows(LANES), rows(LANES), rows(D),
                  pl.BlockSpec((None, 8, N_MOD * D), lambda b, i: (layer, 0, 0)),
                  pl.BlockSpec((1, D), lambda b, i: (0, 0))],
        out_specs=rows(D),
        compiler_params=_cparams(2),
        name="expert_scatter",
    )(ys, slot, aff, x1, mod, fg)


def _state_to_kernel_layout(s):
    B = s.shape[0]
    st = jnp.swapaxes(s, -1, -2).reshape(B, 2, 2, 2, HG_DK, HG_DK)
    eye = jnp.eye(2, dtype=s.dtype)
    full = st[:, :, :, :, :, None, :] * eye[None, None, None, :, None, :, None]
    return full.reshape(B, 2, 2, 2 * HG_DK, 2 * HG_DK)


def _state_from_kernel_layout(st):
    B = st.shape[0]
    full = st.reshape(B, 2, 2, 2, HG_DK, 2, HG_DK)
    diag = jnp.stack([full[:, :, :, h, :, h, :] for h in range(2)], axis=3)
    return jnp.swapaxes(diag.reshape(B, 2, HG_HEADS, HG_DK, HG_DK), -1, -2)


def kernel(x_prompt, x_sample, c, cache_diff_k, cache_diff_v, state_hgrn, c_ctx, norm_g, final_norm_g,
           w_mod, b_mod, w_in, w_out, hgrn_lb_logits, hgrn_norm_g, diff_lambda, diff_subln_g,
           hy_conv_w, hy_conv_b, hy_filt_w1, hy_filt_b1, hy_filt_w2, hy_filt_b2, hy_filt_w3, hy_filt_freq,
           hy_skip, w_router, w_gate, w_up, w_down):
    B_p, L_p, D = x_prompt.shape
    B_s, L_s, _ = x_sample.shape
    cap_p = EC_CAPACITY_FACTOR * L_p // N_EXPERTS
    cap_s = EC_CAPACITY_FACTOR * L_s // N_EXPERTS
    assert B_p * cap_p == B_s * cap_s

    cond8 = jnp.zeros((8, D), F32).at[0].set(c_ctx.astype(F32)).at[1:1 + B_s].set(c.astype(F32))
    mod = _modulation(cond8, w_mod, b_mod)
    groups = (
        dict(cond=(0, 0), cap=cap_p, L=L_p),
        dict(cond=(1, 1), cap=cap_s, L=L_s),
    )
    filt = {L: _hyena_filters(L, hy_filt_w1, hy_filt_b1, hy_filt_w2, hy_filt_b2, hy_filt_w3, hy_filt_freq)
            for L in {L_p, L_s}}
    rope_tabs = _rope_tables(L_s)
    fg = final_norm_g.astype(F32).reshape(1, D)
    wr_pad = jnp.pad(w_router.astype(F32), ((0, 0), (0, 0), (0, LANES - N_EXPERTS)))
    s0 = jnp.zeros((B_p, 2, 2, HG_W // 2, HG_W // 2), F32)
    caches = None
    w_in_bf = w_in.astype(BF16)
    w_out_bf = w_out.astype(BF16)

    xs = [x_prompt.astype(F32), x_sample.astype(F32)]
    new_s = []
    for l in range(DEPTH):
        ng = jnp.tile(hgrn_norm_g[l].astype(F32), HG_HEADS).reshape(1, HG_W)
        sg = diff_subln_g[l].astype(F32).reshape(1, DA_DV)
        g1 = norm_g[l, 0].astype(F32).reshape(1, D)
        g2 = norm_g[l, 1].astype(F32).reshape(1, D)
        x1s, slots, affs, gathered = [], [], [], []
        for gi, grp in enumerate(groups):
            x = xs[gi]
            cb, cs = grp["cond"]
            if gi == 0:
                z, qa, ka, va, kc, vc = _in_proj(x, mod, g1, w_in_bf, cb, cs, l, caches=caches,
                                                 cache_dtype=x_prompt.dtype)
                caches = (kc, vc)
                s_init = s0
            else:
                z, qa, ka, va = _in_proj(x, mod, g1, w_in_bf, cb, cs, l, rope_tabs=rope_tabs)
                s_init = _state_to_kernel_layout(state_hgrn[:, l].astype(F32))
            o_hg, s_fin = _hgrn(z, hgrn_lb_logits.astype(F32), s_init, ng, l)
            if gi == 0:
                new_s.append(_state_from_kernel_layout(s_fin))
                o_da = _attention(qa, ka, va, l, diff_lambda[l].astype(F32), sg)
            else:
                o_da = _attention(qa, ka, va, l, diff_lambda[l].astype(F32), sg,
                                  cache_diff_k.astype(F32), cache_diff_v.astype(F32))
            o_hy = _hyena(z, filt[grp["L"]], l, hy_conv_w[l].astype(F32), hy_conv_b[l].astype(F32),
                          hy_skip[l].astype(F32))
            x1, h2, aff = _out_proj(o_hg, o_da, o_hy, x, mod, g2, w_out_bf, wr_pad, cb, cs, l)
            slot, slot_t = _route(aff, grp["cap"])
            x1s.append(x1)
            slots.append(slot)
            affs.append(aff)
            gathered.append(_gather(slot_t, h2, grp["cap"]))
        ys = _expert_ffn(gathered[0], gathered[1], w_gate, w_up, w_down, l)
        for gi, grp in enumerate(groups):
            cb, cs = grp["cond"]
            xs[gi] = _scatter(ys[gi], slots[gi], affs[gi], x1s[gi], mod, fg, grp["cap"], cb, cs,
                              final=(l == DEPTH - 1), layer=l)

    dt = x_prompt.dtype
    return (xs[0].astype(dt), xs[1].astype(x_sample.dtype),
            caches[0], caches[1], jnp.stack(new_s, axis=1).astype(dt))
```

```python
import functools
import math

import jax
import jax.numpy as jnp
import numpy as np
from jax import lax
from jax.experimental import pallas as pl
from jax.experimental.pallas import tpu as pltpu

F32 = jnp.float32
BF16 = jnp.bfloat16
HI = lax.Precision.HIGHEST

D_MODEL = 1024
DEPTH = 4
GRID_W = 64
HG_HEADS = 4
HG_DK = 64
HG_W = 256
DA_HEADS = 4
DA_DQK = 64
DA_DV = 128
DA_W = 512
HY_W = 256
HY_ORDER = 2
HY_POS_EMB = 33
HY_HIDDEN = 64
HY_SHORT_DECAY_PCT = 0.3
HY_LONG_DECAY_PCT = 1.5
HY_TARGET = 1e-2
N_EXPERTS = 16
EC_CAPACITY_FACTOR = 2
D_FF = 1024
N_MOD = 6
ROPE_BASE = 10000.0
NORM_EPS = 1e-6
D_IN = 3584
COL_HQ, COL_HFF, COL_HFB, COL_HI, COL_HG = 0, 256, 512, 768, 1024
COL_AQ, COL_AK, COL_AV, COL_HY = 1280, 1792, 2304, 2816
Z_W = COL_AQ + (D_IN - COL_HY)
Z_HY = COL_AQ

LANES = 128
HGRN_CHUNK = 16
HGRN_GROUP = 8
GATHER_ROWS = 1024
SHORT_SEQ_ROWS = 1024
CONV_STEP_ROWS = 4096
VMEM_LIMIT = 56 * 1024 * 1024


def _cparams(n_axes):
    return pltpu.CompilerParams(dimension_semantics=("arbitrary",) * n_axes,
                                vmem_limit_bytes=VMEM_LIMIT)


def _requests_per_step(B, L, rows=None, other_steps=1):
    rb = max(1, min(B, (rows or SHORT_SEQ_ROWS) // L))
    while B % rb or (rb > 1 and other_steps * (B // rb) < 2):
        rb -= 1
    return rb


def _dot(a, b):
    return jnp.dot(a.astype(BF16), b.astype(BF16), preferred_element_type=F32)


def _dot_nt(a, b):
    return lax.dot_general(a.astype(BF16), b.astype(BF16), (((1,), (1,)), ((), ())),
                           preferred_element_type=F32)


def _dot_tn(a, b):
    return lax.dot_general(a.astype(BF16), b.astype(BF16), (((0,), (0,)), ((), ())),
                           preferred_element_type=F32)


def _dot_hi(a, b):
    return jnp.dot(a, b, precision=HI, preferred_element_type=F32)


def _sigmoid(x):
    return 1.0 / (1.0 + jnp.exp(-x))


def _silu(x):
    return x * _sigmoid(x)


def _mod_kernel(cond_ref, w_ref, b_ref, o_ref):
    s = _silu(cond_ref[...])
    o_ref[...] = _dot(s, w_ref[...]) + b_ref[...]


def _modulation(cond8, w_mod, b_mod):
    tn = 1536
    n_mod = N_MOD * D_MODEL
    return pl.pallas_call(
        _mod_kernel,
        out_shape=jax.ShapeDtypeStruct((DEPTH, 8, n_mod), F32),
        grid=(DEPTH, n_mod // tn),
        in_specs=[
            pl.BlockSpec((8, D_MODEL), lambda l, j: (0, 0)),
            pl.BlockSpec((None, D_MODEL, tn), lambda l, j: (l, 0, j)),
            pl.BlockSpec((None, 1, tn), lambda l, j: (l, 0, j)),
        ],
        out_specs=pl.BlockSpec((None, 8, tn), lambda l, j: (l, 0, j)),
        compiler_params=_cparams(2),
        name="modulation",
    )(cond8, w_mod, b_mod.reshape(DEPTH, 1, n_mod))


def _rms(x):
    return x * lax.rsqrt(jnp.mean(x * x, axis=-1, keepdims=True) + NORM_EPS)


def _rope(x, cos, sin):
    lane = lax.broadcasted_iota(jnp.int32, x.shape, 1)
    swapped = jnp.where((lane & 31) < 16, -pltpu.roll(x, LANES - 16, 1), pltpu.roll(x, 16, 1))
    return x * cos + swapped * sin


def _in_kernel(cond_base, cond_stride, latent, x_ref, mod_ref, g_ref, w_ref, *rest):
    if latent:
        cos_ref, sin_ref, z_ref, q_ref, k_ref, v_ref = rest
    else:
        z_ref, q_ref, k_ref, v_ref, kc_ref, vc_ref = rest[-6:]
    row = cond_base + cond_stride * pl.program_id(0)
    shift = mod_ref[pl.ds(row, 1), pl.ds(0, D_MODEL)]
    scale = mod_ref[pl.ds(row, 1), pl.ds(D_MODEL, D_MODEL)]
    h = (_rms(x_ref[...]) * g_ref[...] * (1.0 + scale) + shift).astype(BF16)
    z_ref[:, 0:COL_AQ] = jnp.dot(h, w_ref[:, 0:COL_AQ], preferred_element_type=F32)
    z_ref[:, COL_AQ:Z_W] = jnp.dot(h, w_ref[:, COL_HY:D_IN], preferred_element_type=F32)
    att = jnp.dot(h, w_ref[:, COL_AQ:COL_HY], preferred_element_type=F32)
    q = att[:, 0:DA_W]
    k = att[:, DA_W:2 * DA_W]
    v = att[:, 2 * DA_W:3 * DA_W]
    if latent:
        cos = cos_ref[...]
        sin = sin_ref[...]
        heads = [slice(hh * LANES, (hh + 1) * LANES) for hh in range(DA_HEADS)]
        q = jnp.concatenate([_rope(q[:, s], cos, sin) for s in heads], axis=1)
        k = jnp.concatenate([_rope(k[:, s], cos, sin) for s in heads], axis=1)
    else:
        for hh in range(DA_HEADS):
            for m in range(2):
                c0 = (2 * hh + m) * DA_DQK
                kc_ref[hh, m] = k[:, c0:c0 + DA_DQK].astype(kc_ref.dtype)
            vc_ref[hh] = v[:, hh * DA_DV:(hh + 1) * DA_DV].astype(vc_ref.dtype)
    q_ref[...] = (q * (DA_DQK ** -0.5)).astype(BF16)
    k_ref[...] = k.astype(BF16)
    v_ref[...] = v.astype(BF16)


def _in_proj(x, mod, g, w_bf, cond_base, cond_stride, layer, rope_tabs=None, caches=None, cache_dtype=None):
    B, L, _ = x.shape
    tm = 256
    latent = rope_tabs is not None
    rows = lambda w: pl.BlockSpec((None, tm, w), lambda b, i: (b, i, 0))
    in_specs = [
        rows(D_MODEL),
        pl.BlockSpec((None, 8, N_MOD * D_MODEL), lambda b, i: (layer, 0, 0)),
        pl.BlockSpec((1, D_MODEL), lambda b, i: (0, 0)),
        pl.BlockSpec((None, D_MODEL, D_IN), lambda b, i: (layer, 0, 0)),
    ]
    args = [x, mod, g, w_bf]
    out_shape = [jax.ShapeDtypeStruct((B, L, Z_W), F32)] + [jax.ShapeDtypeStruct((B, L, DA_W), BF16)] * 3
    out_specs = [rows(Z_W), rows(DA_W), rows(DA_W), rows(DA_W)]
    aliases = {}
    if latent:
        in_specs += [pl.BlockSpec((tm, LANES), lambda b, i: (i, 0))] * 2
        args += list(rope_tabs)
    else:
        assert tm == L
        out_shape += [jax.ShapeDtypeStruct((B, DEPTH, DA_HEADS, 2, L, DA_DQK), cache_dtype),
                      jax.ShapeDtypeStruct((B, DEPTH, DA_HEADS, L, DA_DV), cache_dtype)]
        out_specs += [pl.BlockSpec((None, None, DA_HEADS, 2, tm, DA_DQK), lambda b, i: (b, layer, 0, 0, i, 0)),
                      pl.BlockSpec((None, None, DA_HEADS, tm, DA_DV), lambda b, i: (b, layer, 0, i, 0))]
        if caches is not None:
            in_specs += [pl.BlockSpec(memory_space=pl.ANY)] * 2
            args += list(caches)
            aliases = {4: 4, 5: 5}
    return pl.pallas_call(
        functools.partial(_in_kernel, cond_base, cond_stride, latent),
        out_shape=tuple(out_shape),
        grid=(B, L // tm),
        in_specs=in_specs,
        out_specs=tuple(out_specs),
        input_output_aliases=aliases,
        compiler_params=_cparams(2),
        name="in_proj",
    )(*args)


def _head_block_mask(n, width):
    r = lax.broadcasted_iota(jnp.int32, (n, n), 0) // width
    c = lax.broadcasted_iota(jnp.int32, (n, n), 1) // width
    return r == c


def _hgrn_kernel(L, layer, has_state, q_ref, ff_ref, fb_ref, v_ref, g_ref, lbl_ref, ng_ref, *rest):
    if has_state:
        s0_ref = rest[0]
        rest = rest[1:]
    o_ref, sfin_ref, qh_s, vh_s, bf_s, bb_s, kf_s, kb_s, of_s, ob_s, st_s = rest
    C = HGRN_CHUNK
    G = HGRN_GROUP
    R = G * C
    W = HG_W
    HALF = W // 2
    ng_groups = L // R

    lg = [lbl_ref[j] for j in range(DEPTH)]
    mx = functools.reduce(jnp.maximum, lg)
    ex = [jnp.exp(a - mx) for a in lg]
    den = functools.reduce(lambda a, b: a + b, ex)
    sm = [e / den for e in ex]
    cum = sm[0]
    for j in range(1, layer + 1):
        cum = cum + sm[j]
    lb = cum - sm[0]

    rowc = lax.broadcasted_iota(jnp.int32, (L, W), 0) & (C - 1)

    def gates(f, lbd):
        e = jnp.exp(-jnp.abs(f))
        r = 1.0 / (1.0 + e)
        log_sig = jnp.minimum(f, 0.0) + jnp.log(r)
        sig_neg = jnp.where(f >= 0.0, e * r, r)
        if layer == 0:
            return sig_neg, log_sig
        a = jnp.log(lbd)
        c = jnp.log1p(-lbd) + log_sig
        logf = jnp.maximum(a, c) + jnp.log1p(jnp.exp(-jnp.abs(a - c)))
        return (1.0 - lbd) * sig_neg, logf

    kf, lf = gates(ff_ref[...], lb[0:1])
    kb, lbk = gates(fb_ref[...], lb[1:2])

    def put(ref, x):
        ref[0] = x[:, 0:HALF]
        ref[1] = x[:, HALF:W]

    def get(ref, rows):
        return jnp.concatenate([ref.at[0][rows, :], ref.at[1][rows, :]], axis=1)

    put(qh_s, q_ref[...])
    put(vh_s, v_ref[...])
    put(kf_s, kf)
    put(kb_s, kb)
    d = 1
    while d < C:
        lf = lf + jnp.where(rowc >= d, pltpu.roll(lf, d, 0), 0.0)
        lbk = lbk + jnp.where(rowc < C - d, pltpu.roll(lbk, L - d, 0), 0.0)
        d *= 2
    put(bf_s, lf)
    put(bb_s, lbk)

    zero = jnp.zeros((HG_DK, HG_DK), F32)
    for d in range(2):
        for p in range(2):
            if has_state:
                s_a, s_b = s0_ref[d, 2 * p], s0_ref[d, 2 * p + 1]
                blocks = jnp.concatenate([jnp.concatenate([s_a, zero], axis=1),
                                          jnp.concatenate([zero, s_b], axis=1)], axis=0)
                st_s[d, p] = jnp.transpose(blocks)
            else:
                st_s[d, p] = jnp.zeros((HALF, HALF), F32)
    bd_bf = jnp.where(_head_block_mask(W, HG_DK), 1.0, 0.0).astype(BF16)
    bd_half = _head_block_mask(HALF, HG_DK)

    def head_sum(p):
        return jnp.dot(p.astype(BF16), bd_bf, preferred_element_type=F32)

    def group(r0, b_s, k_s, o_s, d, forward):
        def pos(ref, s):
            return get(ref, pl.ds(r0 + s, G, stride=C))

        pairs = [(t, s) for s in range(C) for t in (range(s, C) if forward else range(s + 1))]
        prods = [pos(qh_s, t) * pos(k_s, s) if t == s
                 else jnp.exp(pos(b_s, t) - pos(b_s, s)) * pos(qh_s, t) * pos(k_s, s) for t, s in pairs]
        a = head_sum(jnp.concatenate(prods, axis=0))
        acc = [None] * C
        for i, (t, s) in enumerate(pairs):
            term = a[i * G:(i + 1) * G] * pos(vh_s, s)
            acc[t] = term if acc[t] is None else acc[t] + term
        for t in range(C):
            o_s.at[0][pl.ds(r0 + t, G, stride=C), :] = acc[t][:, 0:HALF]
            o_s.at[1][pl.ds(r0 + t, G, stride=C), :] = acc[t][:, HALF:W]
        order = list(range(G)) if forward else list(range(G - 1, -1, -1))
        halves = [slice(p * HALF, (p + 1) * HALF) for p in range(2)]
        qe, kv, dec = {}, {}, {}
        for c in order:
            rows = pl.ds(r0 + c * C, C)
            b = get(b_s, rows)
            v = get(vh_s, rows)
            b_end = b[C - 1:C] if forward else b[0:1]
            qe[c] = get(qh_s, rows) * jnp.exp(b)
            ke = get(k_s, rows) * jnp.exp(b_end - b)
            dec[c] = jnp.exp(b_end)
            kv[c] = [jnp.where(bd_half, _dot_tn(v[:, ln], ke[:, ln]), 0.0) for ln in halves]
        st = [st_s[d, p] for p in range(2)]
        before = {}
        for c in order:
            before[c] = [s.astype(BF16) for s in st]
            st = [st[p] * dec[c][:, halves[p]] + kv[c][p] for p in range(2)]
        for p in range(2):
            st_s[d, p] = st[p]
        for c in order:
            rows = pl.ds(r0 + c * C, C)
            for p in range(2):
                o_s.at[p][rows, :] = o_s.at[p][rows, :] + _dot_nt(qe[c][:, halves[p]], before[c][p])

    def body(i, carry):
        group(pl.multiple_of(i * R, R), bf_s, kf_s, of_s, 0, True)
        group(pl.multiple_of((ng_groups - 1 - i) * R, R), bb_s, kb_s, ob_s, 1, False)
        return carry

    lax.fori_loop(0, ng_groups, body, 0)

    o = jnp.concatenate([of_s[0] + ob_s[0], of_s[1] + ob_s[1]], axis=1)
    ms = _dot_hi(o * o, jnp.where(_head_block_mask(W, HG_DK), 1.0 / HG_DK, 0.0).astype(F32))
    o_ref[...] = o * lax.rsqrt(ms + NORM_EPS) * ng_ref[...] * _silu(g_ref[...])
    for d in range(2):
        for p in range(2):
            blocks = jnp.transpose(st_s[d, p])
            for hl in range(2):
                sfin_ref[d, 2 * p + hl] = blocks[hl * HG_DK:(hl + 1) * HG_DK, hl * HG_DK:(hl + 1) * HG_DK]


def _hgrn(z, lb_logits, ng_tiled, layer, state=None):
    B, L, _ = z.shape
    W = HG_W
    HALF = W // 2

    def col(c):
        return pl.BlockSpec((None, L, W), lambda b, c=c: (b, 0, c // W))

    seq = pltpu.VMEM((2, L, HALF), F32)
    st_shape = (2, HG_HEADS, HG_DK, HG_DK)
    in_specs = [col(COL_HQ), col(COL_HFF), col(COL_HFB), col(COL_HI), col(COL_HG),
                pl.BlockSpec((DEPTH, 2, W), lambda b: (0, 0, 0)),
                pl.BlockSpec((1, W), lambda b: (0, 0))]
    args = [z, z, z, z, z, lb_logits, ng_tiled]
    if state is not None:
        in_specs.append(pl.BlockSpec((None, None) + st_shape, lambda b: (b, layer, 0, 0, 0, 0)))
        args.append(state)
    return pl.pallas_call(
        functools.partial(_hgrn_kernel, L, layer, state is not None),
        out_shape=(jax.ShapeDtypeStruct((B, L, W), F32),
                   jax.ShapeDtypeStruct((B,) + st_shape, F32)),
        grid=(B,),
        in_specs=in_specs,
        out_specs=(pl.BlockSpec((None, L, W), lambda b: (b, 0, 0)),
                   pl.BlockSpec((None,) + st_shape, lambda b: (b, 0, 0, 0, 0))),
        scratch_shapes=[seq] * 8 + [pltpu.VMEM((2, 2, HALF, HALF), F32)],
        compiler_params=_cparams(1),
        name="hgrn",
    )(*args)


def _attn_kernel(latent, lam_init, n_heads, q_ref, k_ref, v_ref, *rest):
    if latent:
        ck_ref, cv_ref, lam_ref, sg_ref, o_ref = rest
    else:
        lam_ref, sg_ref, o_ref = rest
    lv = lam_ref[...]
    lam = (jnp.exp(jnp.sum(lv[0:1] * lv[1:2], axis=1, keepdims=True))
           - jnp.exp(jnp.sum(lv[2:3] * lv[3:4], axis=1, keepdims=True)) + lam_init)
    for hh in range(n_heads):
        hs = slice(hh * LANES, (hh + 1) * LANES)
        q = q_ref[:, hs]
        k = k_ref[:, hs]
        v = v_ref[:, hs]
        lane = lax.broadcasted_iota(jnp.int32, q.shape, 1)
        outs = []
        for m in range(2):
            qm = jnp.where((lane >= m * DA_DQK) & (lane < (m + 1) * DA_DQK), q, jnp.zeros_like(q))
            s = lax.dot_general(qm, k, (((1,), (1,)), ((), ())), preferred_element_type=F32)
            mx = jnp.max(s, axis=1, keepdims=True)
            if latent:
                sc = _dot_nt(q[:, m * DA_DQK:(m + 1) * DA_DQK], ck_ref[hh, m])
                mx = jnp.maximum(mx, jnp.max(sc, axis=1, keepdims=True))
            p = jnp.exp(s - mx)
            den = jnp.sum(p, axis=1, keepdims=True)
            o = jnp.dot(p.astype(BF16), v, preferred_element_type=F32)
            if latent:
                pc = jnp.exp(sc - mx)
                den = den + jnp.sum(pc, axis=1, keepdims=True)
                o = o + _dot(pc, cv_ref[hh])
            outs.append(o / den)
        o = outs[0] - lam * outs[1]
        o_ref[:, hs] = _rms(o) * sg_ref[...] * (1.0 - lam_init)


def _attention(q, k, v, layer, lam_p, subln_g, ctx_k=None, ctx_v=None):
    B, L, _ = q.shape
    latent = ctx_k is not None
    lam_init = 0.8 - 0.6 * math.exp(-0.3 * layer)
    if latent:
        tq, nh = 256, 4
        grid = (B, DA_HEADS // nh, L // tq)
        P = ctx_k.shape[4]
        in_specs = [
            pl.BlockSpec((None, tq, nh * LANES), lambda b, h, i: (b, i, h)),
            pl.BlockSpec((None, L, nh * LANES), lambda b, h, i: (b, 0, h)),
            pl.BlockSpec((None, L, nh * LANES), lambda b, h, i: (b, 0, h)),
            pl.BlockSpec((None, None, nh, 2, P, DA_DQK), lambda b, h, i: (b, layer, h, 0, 0, 0)),
            pl.BlockSpec((None, None, nh, P, DA_DV), lambda b, h, i: (b, layer, h, 0, 0)),
            pl.BlockSpec((4, DA_DQK), lambda b, h, i: (0, 0)),
            pl.BlockSpec((1, DA_DV), lambda b, h, i: (0, 0)),
        ]
        args = [q, k, v, ctx_k, ctx_v, lam_p, subln_g]
        out_spec = pl.BlockSpec((None, tq, nh * DA_DV), lambda b, h, i: (b, i, h))
    else:
        nh = DA_HEADS
        grid = (B,)
        full = pl.BlockSpec((None, L, DA_W), lambda b: (b, 0, 0))
        in_specs = [full, full, full,
                    pl.BlockSpec((4, DA_DQK), lambda b: (0, 0)),
                    pl.BlockSpec((1, DA_DV), lambda b: (0, 0))]
        args = [q, k, v, lam_p, subln_g]
        out_spec = full
    return pl.pallas_call(
        functools.partial(_attn_kernel, latent, lam_init, nh),
        out_shape=jax.ShapeDtypeStruct((B, L, DA_W), F32),
        grid=grid,
        in_specs=in_specs,
        out_specs=out_spec,
        compiler_params=_cparams(len(grid)),
        name="diff_attention",
    )(*args)


def _rope_tables(L):
    half = DA_DQK // 2
    inv = ROPE_BASE ** (-jnp.arange(0, half, 2, dtype=F32) / half)
    t = jnp.arange(L)
    rows = (t // GRID_W).astype(F32)[:, None] * inv[None, :]
    cols = (t % GRID_W).astype(F32)[:, None] * inv[None, :]
    ang = jnp.concatenate([rows, rows, cols, cols], axis=-1)
    ang = jnp.concatenate([ang, ang], axis=-1)
    return jnp.cos(ang), jnp.sin(ang)


@functools.lru_cache(maxsize=None)
def _dft_np(L):
    idx = (np.arange(L, dtype=np.int64)[:, None] * np.arange(L, dtype=np.int64)[None, :]) % (2 * L)
    ang = idx.astype(np.float64) * (math.pi / L)
    return np.cos(ang).astype(np.float32), np.sin(ang).astype(np.float32)


def _dft_mats(L):
    c, s = _dft_np(L)
    return jnp.asarray(c).astype(BF16), jnp.asarray(s).astype(BF16)


def _filt_hidden_kernel(z_ref, w1_ref, b1_ref, w2_ref, b2_ref, fr_ref, hid_ref):
    fr = fr_ref[...]
    hid = jnp.sin(fr[0:1] * (_dot_hi(z_ref[...], w1_ref[...]) + b1_ref[...]))
    hid_ref[...] = jnp.sin(fr[1:2] * (_dot_hi(hid, w2_ref[...]) + b2_ref[...]))


def _filt_taps_kernel(z_ref, hid_ref, w3f_ref, w3b_ref, dl_ref, sum_ref, diff_ref, nyq_ref):
    hid = hid_ref[...]
    decay = jnp.exp(-z_ref[:, 0:1] * jnp.abs(dl_ref[...]))
    hf = _dot_hi(hid, w3f_ref[...]) * decay
    hb = _dot_hi(hid, w3b_ref[...]) * decay
    nrm = (jnp.sum(jnp.abs(hf), axis=0, keepdims=True)
           + jnp.sum(jnp.abs(hb), axis=0, keepdims=True))
    row = lax.broadcasted_iota(jnp.int32, hf.shape, 0)
    hf = hf / nrm
    hb = jnp.where(row == 0, 0.0, hb / nrm)
    fs = hf + hb
    sum_ref[...] = fs.astype(BF16)
    diff_ref[...] = (hf - hb).astype(BF16)
    sgn = jnp.where((row & 1) == 0, 1.0, -1.0)
    nyq_ref[...] = jnp.broadcast_to(jnp.sum(fs * sgn, axis=0, keepdims=True), nyq_ref.shape)


def _filt_spec_kernel(c_ref, s_ref, sum_ref, diff_ref, nyq_ref, a_ref, b_ref, d_ref):
    a = _dot(c_ref[...], sum_ref[...])
    him = -_dot(s_ref[...], diff_ref[...])
    row = lax.broadcasted_iota(jnp.int32, a.shape, 0) + pl.program_id(0) * a.shape[0]
    a_ref[...] = a
    b_ref[...] = jnp.where(row == 0, 0.0, him)
    d_ref[...] = jnp.where(row == 0, nyq_ref[0:1], a)


def _hyena_filters(L, w1, b1, w2, b2, w3, freq):
    t = jnp.linspace(0.0, 1.0, L, dtype=F32)[:, None]
    bands = (HY_POS_EMB - 1) // 2
    fr = jnp.linspace(1e-4, bands - 1, bands, dtype=F32)[None, :]
    w = 2.0 * math.pi * jnp.arange(L, dtype=F32)[:, None] / L
    z = jnp.concatenate([t, jnp.cos(fr * w), -jnp.sin(fr * w)], axis=-1)
    z = jnp.pad(z, ((0, 0), (0, HY_HIDDEN - HY_POS_EMB)))
    w1p = jnp.pad(w1, ((0, 0), (0, HY_HIDDEN - HY_POS_EMB), (0, 0)))
    max_decay = math.log(HY_TARGET) / HY_SHORT_DECAY_PCT
    min_decay = math.log(HY_TARGET) / HY_LONG_DECAY_PCT
    deltas = jnp.linspace(min_decay, max_decay, HY_W, dtype=F32)[None, :]
    tc = LANES
    nct = HY_W // tc
    nw = HY_ORDER * HY_W
    hid = pl.pallas_call(
        _filt_hidden_kernel,
        out_shape=jax.ShapeDtypeStruct((DEPTH, L, HY_HIDDEN), F32),
        grid=(DEPTH,),
        in_specs=[
            pl.BlockSpec((L, HY_HIDDEN), lambda l: (0, 0)),
            pl.BlockSpec((None, HY_HIDDEN, HY_HIDDEN), lambda l: (l, 0, 0)),
            pl.BlockSpec((None, 1, HY_HIDDEN), lambda l: (l, 0, 0)),
            pl.BlockSpec((None, HY_HIDDEN, HY_HIDDEN), lambda l: (l, 0, 0)),
            pl.BlockSpec((None, 1, HY_HIDDEN), lambda l: (l, 0, 0)),
            pl.BlockSpec((None, 2, HY_HIDDEN), lambda l: (l, 0, 0)),
        ],
        out_specs=pl.BlockSpec((None, L, HY_HIDDEN), lambda l: (l, 0, 0)),
        compiler_params=_cparams(1),
        name="hyena_filter_hidden",
    )(z, w1p, b1.reshape(DEPTH, 1, HY_HIDDEN), w2, b2.reshape(DEPTH, 1, HY_HIDDEN), freq)
    fsum, fdiff, nyq = pl.pallas_call(
        _filt_taps_kernel,
        out_shape=(jax.ShapeDtypeStruct((DEPTH, L, nw), BF16),
                   jax.ShapeDtypeStruct((DEPTH, L, nw), BF16),
                   jax.ShapeDtypeStruct((DEPTH, 8, nw), F32)),
        grid=(DEPTH, HY_ORDER, nct),
        in_specs=[
            pl.BlockSpec((L, HY_HIDDEN), lambda l, o, j: (0, 0)),
            pl.BlockSpec((None, L, HY_HIDDEN), lambda l, o, j: (l, 0, 0)),
            pl.BlockSpec((None, HY_HIDDEN, tc), lambda l, o, j: (l, 0, o * nct + j)),
            pl.BlockSpec((None, HY_HIDDEN, tc), lambda l, o, j: (l, 0, (HY_ORDER + o) * nct + j)),
            pl.BlockSpec((1, tc), lambda l, o, j: (0, j)),
        ],
        out_specs=(pl.BlockSpec((None, L, tc), lambda l, o, j: (l, 0, o * nct + j)),
                   pl.BlockSpec((None, L, tc), lambda l, o, j: (l, 0, o * nct + j)),
                   pl.BlockSpec((None, 8, tc), lambda l, o, j: (l, 0, o * nct + j))),
        compiler_params=_cparams(3),
        name="hyena_filter_taps",
    )(z, hid, w3, w3, deltas)

    cm, sm = _dft_mats(L)
    tf = min(L, 512)
    spec = jax.ShapeDtypeStruct((DEPTH, L, nw), F32)
    return pl.pallas_call(
        _filt_spec_kernel,
        out_shape=(spec, spec, spec),
        grid=(L // tf, DEPTH),
        in_specs=[
            pl.BlockSpec((tf, L), lambda i, l: (i, 0)),
            pl.BlockSpec((tf, L), lambda i, l: (i, 0)),
            pl.BlockSpec((None, L, nw), lambda i, l: (l, 0, 0)),
            pl.BlockSpec((None, L, nw), lambda i, l: (l, 0, 0)),
            pl.BlockSpec((None, 8, nw), lambda i, l: (l, 0, 0)),
        ],
        out_specs=(pl.BlockSpec((None, tf, nw), lambda i, l: (l, i, 0)),) * 3,
        compiler_params=_cparams(2),
        name="hyena_filter_spectrum",
    )(cm, sm, fsum, fdiff, nyq)


def _short_conv_kernel(L, za_ref, zb_ref, zc_ref, w_ref, b_ref, o_ref):
    W = HY_W
    for r in range(za_ref.shape[0]):
        for j, z_ref in enumerate((za_ref, zb_ref, zc_ref)):
            z = z_ref[r]
            w = w_ref[:, j * W:(j + 1) * W]
            row = lax.broadcasted_iota(jnp.int32, z.shape, 0)
            prev = jnp.where(row == 0, 0.0, pltpu.roll(z, 1, 0))
            nxt = jnp.where(row == L - 1, 0.0, pltpu.roll(z, L - 1, 0))
            o_ref[r, :, j * W:(j + 1) * W] = (b_ref[:, j * W:(j + 1) * W] + prev * w[0:1] + z * w[1:2]
                                              + nxt * w[2:3])


def _short_conv(z, conv_w, conv_b):
    B, L, _ = z.shape
    W = HY_W
    nb = (HY_ORDER + 1)
    cb = Z_HY // W
    rb = _requests_per_step(B, L)
    return pl.pallas_call(
        functools.partial(_short_conv_kernel, L),
        out_shape=jax.ShapeDtypeStruct((B, L, nb * W), F32),
        grid=(B // rb,),
        in_specs=[pl.BlockSpec((rb, L, W), lambda b, j=j: (b, 0, cb + j)) for j in range(nb)]
        + [pl.BlockSpec((3, nb * W), lambda b: (0, 0)),
           pl.BlockSpec((1, nb * W), lambda b: (0, 0))],
        out_specs=pl.BlockSpec((rb, L, nb * W), lambda b: (b, 0, 0)),
        compiler_params=_cparams(1),
        name="hyena_short_conv",
    )(z, z, z, conv_w, conv_b.reshape(1, nb * W))


def _conv_fwd_kernel(u_ref, c_ref, s_ref, a_ref, b_ref, d_ref, yre_ref, yim_ref):
    a = a_ref[...]
    bm = b_ref[...]
    dm = d_ref[...]
    cm = c_ref[...]
    sm = s_ref[...]
    for r in range(u_ref.shape[0]):
        u = u_ref[r]
        ure = _dot(cm, u)
        uim = -_dot(sm, u)
        rowu = lax.broadcasted_iota(jnp.int32, u.shape, 0)
        nyq = jnp.sum(jnp.where((rowu & 1) == 0, u, -u), axis=0, keepdims=True)
        row = lax.broadcasted_iota(jnp.int32, ure.shape, 0) + pl.program_id(0) * ure.shape[0]
        uim = jnp.where(row == 0, nyq, uim)
        yre_ref[r] = (ure * a - uim * bm).astype(BF16)
        yim_ref[r] = (ure * bm + uim * dm).astype(BF16)


def _conv_inv_kernel(L, yre_ref, yim_ref, c_ref, s_ref, u_ref, skip_ref, m_ref, o_ref):
    cm = c_ref[...]
    sm = s_ref[...]
    for r in range(u_ref.shape[0]):
        yre = yre_ref[r]
        yim = yim_ref[r]
        y = (jnp.dot(cm, yre, preferred_element_type=F32)
             - jnp.dot(sm, yim, preferred_element_type=F32)) * (1.0 / L)
        row = lax.broadcasted_iota(jnp.int32, y.shape, 0) + pl.program_id(0) * y.shape[0]
        sgn = jnp.where((row & 1) == 0, 1.0, -1.0)
        y = y + (sgn * yim[0:1].astype(F32) - yre[0:1].astype(F32)) * (0.5 / L)
        o_ref[r] = m_ref[r] * (y + u_ref[r] * skip_ref[...])


def _long_conv(zc, u_col, m_col, filt, layer, order, skip, u_arr=None):
    B, L, _ = zc.shape
    W = HY_W
    cm, sm = _dft_mats(L)
    fa, fb, fd = filt
    tf = min(L, 512)
    rb = _requests_per_step(B, L, CONV_STEP_ROWS, L // tf)
    if u_arr is None:
        u_arr, ub = zc, u_col
    else:
        ub = 0
    spec_t = pl.BlockSpec((tf, L), lambda i, b: (i, 0))
    fspec = pl.BlockSpec((None, tf, W), lambda i, b: (layer, i, order))
    yre, yim = pl.pallas_call(
        _conv_fwd_kernel,
        out_shape=(jax.ShapeDtypeStruct((B, L, W), BF16),) * 2,
        grid=(L // tf, B // rb),
        in_specs=[pl.BlockSpec((rb, L, W), lambda i, b: (b, 0, ub)),
                  spec_t, spec_t, fspec, fspec, fspec],
        out_specs=(pl.BlockSpec((rb, tf, W), lambda i, b: (b, i, 0)),) * 2,
        compiler_params=_cparams(2),
        name="hyena_conv_fwd",
    )(u_arr, cm, sm, fa, fb, fd)
    full = pl.BlockSpec((rb, L, W), lambda i, b: (b, 0, 0))
    return pl.pallas_call(
        functools.partial(_conv_inv_kernel, L),
        out_shape=jax.ShapeDtypeStruct((B, L, W), F32),
        grid=(L // tf, B // rb),
        in_specs=[full, full, spec_t, spec_t,
                  pl.BlockSpec((rb, tf, W), lambda i, b: (b, i, ub)),
                  pl.BlockSpec((None, 1, W), lambda i, b: (order, 0, 0)),
                  pl.BlockSpec((rb, tf, W), lambda i, b: (b, i, m_col))],
        out_specs=pl.BlockSpec((rb, tf, W), lambda i, b: (b, i, 0)),
        compiler_params=_cparams(2),
        name="hyena_conv_inv",
    )(yre, yim, cm, sm, u_arr, skip.reshape(HY_ORDER, 1, W), zc)


def _hyena(z, filt, layer, conv_w, conv_b, skip):
    zc = _short_conv(z, conv_w, conv_b)
    y1 = _long_conv(zc, 0, 1, filt, layer, 0, skip)
    return _long_conv(zc, 0, 2, filt, layer, 1, skip, u_arr=y1)


def _out_kernel(cond_base, cond_stride, hg_ref, da_ref, hy_ref, x_ref, mod_ref, g_ref, w_ref, wr_ref,
                x1_ref, h2_ref, aff_ref):
    row = cond_base + cond_stride * pl.program_id(0)
    D = D_MODEL
    gate1 = mod_ref[pl.ds(row, 1), pl.ds(2 * D, D)]
    shift2 = mod_ref[pl.ds(row, 1), pl.ds(3 * D, D)]
    scale2 = mod_ref[pl.ds(row, 1), pl.ds(4 * D, D)]
    mix = (_dot(hg_ref[...], w_ref[0:HG_W, :])
           + _dot(da_ref[...], w_ref[HG_W:HG_W + DA_W, :])
           + _dot(hy_ref[...], w_ref[HG_W + DA_W:D, :]))
    x1 = x_ref[...] + gate1 * mix
    x1_ref[...] = x1
    h2 = _rms(x1) * g_ref[...] * (1.0 + scale2) + shift2
    h2_ref[...] = h2.astype(BF16)
    wr = wr_ref[...]
    h_hi = h2.astype(BF16)
    w_hi = wr.astype(BF16)
    h_lo = (h2 - h_hi.astype(F32)).astype(BF16)
    w_lo = (wr - w_hi.astype(F32)).astype(BF16)
    logits = (jnp.dot(h_hi, w_hi, preferred_element_type=F32) + jnp.dot(h_hi, w_lo, preferred_element_type=F32)
              + jnp.dot(h_lo, w_hi, preferred_element_type=F32))
    lane = lax.broadcasted_iota(jnp.int32, logits.shape, 1)
    logits = jnp.where(lane < N_EXPERTS, logits, -jnp.inf)
    e = jnp.exp(logits - jnp.max(logits, axis=1, keepdims=True))
    aff_ref[...] = e / jnp.sum(e, axis=1, keepdims=True)


def _out_proj(o_hg, o_da, o_hy, x, mod, g, w_bf, wr_pad, cond_base, cond_stride, layer):
    B, L, D = x.shape
    tm = min(L, 512)

    def rows(w):
        return pl.BlockSpec((None, tm, w), lambda b, i: (b, i, 0))

    return pl.pallas_call(
        functools.partial(_out_kernel, cond_base, cond_stride),
        out_shape=(jax.ShapeDtypeStruct((B, L, D), F32),
                   jax.ShapeDtypeStruct((B, L, D), BF16),
                   jax.ShapeDtypeStruct((B, L, LANES), F32)),
        grid=(B, L // tm),
        in_specs=[rows(HG_W), rows(DA_W), rows(HY_W), rows(D),
                  pl.BlockSpec((None, 8, N_MOD * D), lambda b, i: (layer, 0, 0)),
                  pl.BlockSpec((1, D), lambda b, i: (0, 0)),
                  pl.BlockSpec((None, D, D), lambda b, i: (layer, 0, 0)),
                  pl.BlockSpec((None, D, LANES), lambda b, i: (layer, 0, 0))],
        out_specs=(rows(D), rows(D), rows(LANES)),
        compiler_params=_cparams(2),
        name="out_proj",
    )(o_hg, o_da, o_hy, x, mod, g, w_bf, wr_pad)


def _prefix_count(m, blk):
    L = m.shape[0]
    r = lax.broadcasted_iota(jnp.int32, (blk, blk), 0)
    c = lax.broadcasted_iota(jnp.int32, (blk, blk), 1)
    tri = jnp.where(c < r, 1.0, 0.0).astype(BF16)
    carry = jnp.zeros((1, m.shape[1]), F32)
    parts = []
    for i in range(L // blk):
        mb = m[i * blk:(i + 1) * blk]
        parts.append(jnp.dot(tri, mb.astype(BF16), preferred_element_type=F32) + carry)
        carry = carry + jnp.sum(mb, axis=0, keepdims=True)
    return jnp.concatenate(parts, axis=0) if len(parts) > 1 else parts[0]


def _route_kernel(cap, aff_ref, slot_ref, slot_t_ref):
    for r in range(aff_ref.shape[0]):
        _route_one(cap, aff_ref.at[r], slot_ref.at[r], slot_t_ref.at[r])


def _route_one(cap, aff_ref, slot_ref, slot_t_ref):
    aff = aff_ref[...]
    capf = float(cap)
    tiny = 2.0 ** -126

    def enough(t):
        return jnp.sum(jnp.where(aff >= t, 1.0, 0.0), axis=0, keepdims=True) >= capf

    has = enough(jnp.full((1, LANES), tiny, F32))
    lo = jnp.full((1, LANES), tiny, F32)
    for s in (64, 32, 16, 8, 4, 2, 1):
        cand = lo * (2.0 ** s)
        lo = jnp.where(enough(cand), cand, lo)
    step = lo
    for _ in range(23):
        step = step * 0.5
        cand = lo + step
        lo = jnp.where(enough(cand), cand, lo)
    hi = jnp.where(has, lo + jnp.maximum(step, tiny), tiny)
    lo = jnp.where(has, lo, 0.0)
    gt = jnp.where(aff >= hi, 1.0, 0.0)
    eq = jnp.where((aff >= lo) & (aff < hi), 1.0, 0.0)
    need = capf - jnp.sum(gt, axis=0, keepdims=True)
    blk = min(aff.shape[0], 256)
    sel = gt + eq * jnp.where(_prefix_count(eq, blk) < need, 1.0, 0.0)
    lane = lax.broadcasted_iota(jnp.int32, aff.shape, 1)
    sel = jnp.where(lane < N_EXPERTS, sel, 0.0)
    slot = jnp.where(sel > 0.0, _prefix_count(sel, blk), -1.0)
    slot_ref[...] = slot
    slot_t_ref[...] = jnp.transpose(slot)[0:N_EXPERTS, :]


def _route(aff, cap):
    B, L, _ = aff.shape
    rb = _requests_per_step(B, L)
    return pl.pallas_call(
        functools.partial(_route_kernel, cap),
        out_shape=(jax.ShapeDtypeStruct((B, L, LANES), F32),
                   jax.ShapeDtypeStruct((B, N_EXPERTS, L), F32)),
        grid=(B // rb,),
        in_specs=[pl.BlockSpec((rb, L, LANES), lambda b: (b, 0, 0))],
        out_specs=(pl.BlockSpec((rb, L, LANES), lambda b: (b, 0, 0)),
                   pl.BlockSpec((rb, N_EXPERTS, L), lambda b: (b, 0, 0))),
        compiler_params=_cparams(1),
        name="route",
    )(aff)


def _gather_kernel(cap, eg, slot_t_ref, h_ref, xs_ref):
    e0 = pl.program_id(1) * eg
    c = lax.broadcasted_iota(jnp.int32, (cap, slot_t_ref.shape[1]), 0).astype(F32)
    onehot = jnp.concatenate(
        [jnp.where(c == slot_t_ref[pl.ds(e0 + j, 1), :], 1.0, 0.0) for j in range(eg)],
        axis=0).astype(BF16)
    xs = jnp.dot(onehot, h_ref[...], preferred_element_type=F32).astype(BF16)
    for j in range(eg):
        xs_ref[j] = xs[j * cap:(j + 1) * cap]


def _gather(slot_t, h2, cap):
    B, L, D = h2.shape
    eg = max(1, min(N_EXPERTS, GATHER_ROWS // cap))
    return pl.pallas_call(
        functools.partial(_gather_kernel, cap, eg),
        out_shape=jax.ShapeDtypeStruct((N_EXPERTS, B * cap, D), BF16),
        grid=(B, N_EXPERTS // eg),
        in_specs=[pl.BlockSpec((None, N_EXPERTS, L), lambda b, e: (b, 0, 0)),
                  pl.BlockSpec((None, L, D), lambda b, e: (b, 0, 0))],
        out_specs=pl.BlockSpec((eg, cap, D), lambda b, e: (e, b, 0)),
        compiler_params=_cparams(2),
        name="expert_gather",
    )(slot_t, h2)


def _ffn_kernel(xa_ref, xb_ref, wg_ref, wu_ref, wd_ref, oa_ref, ob_ref):
    wg = wg_ref[...].astype(BF16)
    wu = wu_ref[...].astype(BF16)
    wd = wd_ref[...].astype(BF16)
    for x_ref, o_ref in ((xa_ref, oa_ref), (xb_ref, ob_ref)):
        x = x_ref[...]
        hid = (_silu(jnp.dot(x, wg, preferred_element_type=F32))
               * jnp.dot(x, wu, preferred_element_type=F32))
        o_ref[...] = jnp.dot(hid.astype(BF16), wd, preferred_element_type=F32).astype(BF16)


def _expert_ffn(xs_a, xs_b, w_gate, w_up, w_down, layer):
    E, S, D = xs_a.shape
    xspec = pl.BlockSpec((None, S, D), lambda e: (e, 0, 0))
    out = jax.ShapeDtypeStruct((E, S, D), BF16)
    return pl.pallas_call(
        _ffn_kernel,
        out_shape=(out, out),
        grid=(E,),
        in_specs=[xspec, xspec,
                  pl.BlockSpec((None, None, D, D_FF), lambda e: (layer, e, 0, 0)),
                  pl.BlockSpec((None, None, D, D_FF), lambda e: (layer, e, 0, 0)),
                  pl.BlockSpec((None, None, D_FF, D), lambda e: (layer, e, 0, 0))],
        out_specs=(xspec, xspec),
        compiler_params=_cparams(1),
        name="expert_ffn",
    )(xs_a, xs_b, w_gate, w_up, w_down)


def _scatter_kernel(cap, eg, cond_base, cond_stride, final, ys_ref, slot_ref, aff_ref, x_ref, mod_ref, fg_ref,
                    o_ref):
    row = cond_base + cond_stride * pl.program_id(0)
    gate2 = mod_ref[pl.ds(row, 1), pl.ds(5 * D_MODEL, D_MODEL)]
    slot = slot_ref[...].astype(BF16)
    aff = aff_ref[...].astype(BF16)
    tt = slot.shape[0]
    K = eg * cap
    shift = cap.bit_length() - 1
    col_expert = lax.broadcasted_iota(jnp.int32, (LANES, K), 1) >> shift
    lane_expert = lax.broadcasted_iota(jnp.int32, (LANES, K), 0)
    col_slot = (lax.broadcasted_iota(jnp.int32, (tt, K), 1) & (cap - 1)).astype(F32)
    y = jnp.zeros((tt, D_MODEL), F32)
    for g in range(N_EXPERTS // eg):
        expand = jnp.where(lane_expert == col_expert + g * eg, 1.0, 0.0).astype(BF16)
        slot_x = jnp.dot(slot, expand, preferred_element_type=F32)
        aff_x = jnp.dot(aff, expand, preferred_element_type=F32)
        w = jnp.where(slot_x == col_slot, aff_x, 0.0).astype(BF16)
        ys = jnp.concatenate([ys_ref[g * eg + j] for j in range(eg)], axis=0)
        y = y + jnp.dot(w, ys, preferred_element_type=F32)
    x2 = x_ref[...] + gate2 * y
    if final:
        x2 = _rms(x2) * fg_ref[...]
    o_ref[...] = x2


def _scatter(ys, slot, aff, x1, mod, fg, cap, cond_base, cond_stride, final, layer):
    B, L, D = x1.shape
    tt = min(L, 1024)
    assert cap & (cap - 1) == 0
    eg = max(1, min(N_EXPERTS, GATHER_ROWS // cap))

    def rows(w):
        return pl.BlockSpec((None, tt, w), lambda b, i: (b, i, 0))

    return pl.pallas_call(
        functools.partial(_scatter_kernel, cap, eg, cond_base, cond_stride, final),
        out_shape=jax.ShapeDtypeStruct((B, L, D), F32),
        grid=(B, L // tt),
        in_specs=[pl.BlockSpec((N_EXPERTS, cap, D), lambda b, i: (0, b, 0)),
                  rows(LANES), rows(LANES), rows(D),
                  pl.BlockSpec((None, 8, N_MOD * D), lambda b, i: (layer, 0, 0)),
                  pl.BlockSpec((1, D), lambda b, i: (0, 0))],
        out_specs=rows(D),
        compiler_params=_cparams(2),
        name="expert_scatter",
    )(ys, slot, aff, x1, mod, fg)


def kernel(x_prompt, x_sample, c, cache_diff_k, cache_diff_v, state_hgrn, c_ctx, norm_g, final_norm_g,
           w_mod, b_mod, w_in, w_out, hgrn_lb_logits, hgrn_norm_g, diff_lambda, diff_subln_g,
           hy_conv_w, hy_conv_b, hy_filt_w1, hy_filt_b1, hy_filt_w2, hy_filt_b2, hy_filt_w3, hy_filt_freq,
           hy_skip, w_router, w_gate, w_up, w_down):
    B_p, L_p, D = x_prompt.shape
    B_s, L_s, _ = x_sample.shape
    cap_p = EC_CAPACITY_FACTOR * L_p // N_EXPERTS
    cap_s = EC_CAPACITY_FACTOR * L_s // N_EXPERTS
    assert B_p * cap_p == B_s * cap_s

    cond8 = jnp.zeros((8, D), F32).at[0].set(c_ctx.astype(F32)).at[1:1 + B_s].set(c.astype(F32))
    mod = _modulation(cond8, w_mod, b_mod)
    groups = (
        dict(cond=(0, 0), cap=cap_p, L=L_p),
        dict(cond=(1, 1), cap=cap_s, L=L_s),
    )
    filt = {L: _hyena_filters(L, hy_filt_w1, hy_filt_b1, hy_filt_w2, hy_filt_b2, hy_filt_w3, hy_filt_freq)
            for L in {L_p, L_s}}
    rope_tabs = _rope_tables(L_s)
    fg = final_norm_g.astype(F32).reshape(1, D)
    wr_pad = jnp.pad(w_router.astype(F32), ((0, 0), (0, 0), (0, LANES - N_EXPERTS)))
    caches = None
    w_in_bf = w_in.astype(BF16)
    w_out_bf = w_out.astype(BF16)

    xs = [x_prompt.astype(F32), x_sample.astype(F32)]
    new_s = []
    for l in range(DEPTH):
        ng = jnp.tile(hgrn_norm_g[l].astype(F32), HG_HEADS).reshape(1, HG_W)
        sg = diff_subln_g[l].astype(F32).reshape(1, DA_DV)
        g1 = norm_g[l, 0].astype(F32).reshape(1, D)
        g2 = norm_g[l, 1].astype(F32).reshape(1, D)
        x1s, slots, affs, gathered = [], [], [], []
        for gi, grp in enumerate(groups):
            x = xs[gi]
            cb, cs = grp["cond"]
            if gi == 0:
                z, qa, ka, va, kc, vc = _in_proj(x, mod, g1, w_in_bf, cb, cs, l, caches=caches,
                                                 cache_dtype=x_prompt.dtype)
                caches = (kc, vc)
                o_hg, s_fin = _hgrn(z, hgrn_lb_logits.astype(F32), ng, l)
                new_s.append(s_fin)
            else:
                z, qa, ka, va = _in_proj(x, mod, g1, w_in_bf, cb, cs, l, rope_tabs=rope_tabs)
                o_hg, _ = _hgrn(z, hgrn_lb_logits.astype(F32), ng, l, state=state_hgrn.astype(F32))
            if gi == 0:
                o_da = _attention(qa, ka, va, l, diff_lambda[l].astype(F32), sg)
            else:
                o_da = _attention(qa, ka, va, l, diff_lambda[l].astype(F32), sg,
                                  cache_diff_k.astype(F32), cache_diff_v.astype(F32))
            o_hy = _hyena(z, filt[grp["L"]], l, hy_conv_w[l].astype(F32), hy_conv_b[l].astype(F32),
                          hy_skip[l].astype(F32))
            x1, h2, aff = _out_proj(o_hg, o_da, o_hy, x, mod, g2, w_out_bf, wr_pad, cb, cs, l)
            slot, slot_t = _route(aff, grp["cap"])
            x1s.append(x1)
            slots.append(slot)
            affs.append(aff)
            gathered.append(_gather(slot_t, h2, grp["cap"]))
        ys = _expert_ffn(gathered[0], gathered[1], w_gate, w_up, w_down, l)
        for gi, grp in enumerate(groups):
            cb, cs = grp["cond"]
            xs[gi] = _scatter(ys[gi], slots[gi], affs[gi], x1s[gi], mod, fg, grp["cap"], cb, cs,
                              final=(l == DEPTH - 1), layer=l)

    dt = x_prompt.dtype
    return (xs[0].astype(dt), xs[1].astype(x_sample.dtype),
            caches[0], caches[1], jnp.stack(new_s, axis=1).astype(dt))
```

```python
import functools
import math

import jax
import jax.numpy as jnp
import numpy as np
from jax import lax
from jax.experimental import pallas as pl
from jax.experimental.pallas import tpu as pltpu

F32 = jnp.float32
BF16 = jnp.bfloat16
HI = lax.Precision.HIGHEST

D_MODEL = 1024
DEPTH = 4
GRID_W = 64
HG_HEADS = 4
HG_DK = 64
HG_W = 256
DA_HEADS = 4
DA_DQK = 64
DA_DV = 128
DA_W = 512
HY_W = 256
HY_ORDER = 2
HY_POS_EMB = 33
HY_HIDDEN = 64
HY_SHORT_DECAY_PCT = 0.3
HY_LONG_DECAY_PCT = 1.5
HY_TARGET = 1e-2
N_EXPERTS = 16
EC_CAPACITY_FACTOR = 2
D_FF = 1024
N_MOD = 6
ROPE_BASE = 10000.0
NORM_EPS = 1e-6
D_IN = 3584
COL_HQ, COL_HFF, COL_HFB, COL_HI, COL_HG = 0, 256, 512, 768, 1024
COL_AQ, COL_AK, COL_AV, COL_HY = 1280, 1792, 2304, 2816
Z_W = COL_AQ + (D_IN - COL_HY)
Z_HY = COL_AQ

LANES = 128
HGRN_CHUNK = 16
HGRN_GROUP = 8
GATHER_ROWS = 1024
SHORT_SEQ_ROWS = 1024
CONV_STEP_ROWS = 4096
OUT_SUB_ROWS = 128
VMEM_LIMIT = 56 * 1024 * 1024


def _cparams(n_axes):
    return pltpu.CompilerParams(dimension_semantics=("arbitrary",) * n_axes,
                                vmem_limit_bytes=VMEM_LIMIT)


def _requests_per_step(B, L, rows=None, other_steps=1):
    rb = max(1, min(B, (rows or SHORT_SEQ_ROWS) // L))
    while B % rb or (rb > 1 and other_steps * (B // rb) < 2):
        rb -= 1
    return rb


def _dot(a, b):
    return jnp.dot(a.astype(BF16), b.astype(BF16), preferred_element_type=F32)


def _dot_nt(a, b):
    return lax.dot_general(a.astype(BF16), b.astype(BF16), (((1,), (1,)), ((), ())),
                           preferred_element_type=F32)


def _dot_tn(a, b):
    return lax.dot_general(a.astype(BF16), b.astype(BF16), (((0,), (0,)), ((), ())),
                           preferred_element_type=F32)


def _dot_hi(a, b):
    return jnp.dot(a, b, precision=HI, preferred_element_type=F32)


def _sigmoid(x):
    return 1.0 / (1.0 + jnp.exp(-x))


def _silu(x):
    return x * _sigmoid(x)


def _mod_kernel(cond_ref, w_ref, b_ref, o_ref):
    s = _silu(cond_ref[...])
    o_ref[...] = _dot(s, w_ref[...]) + b_ref[...]


def _modulation(cond8, w_mod, b_mod):
    tn = 1536
    n_mod = N_MOD * D_MODEL
    return pl.pallas_call(
        _mod_kernel,
        out_shape=jax.ShapeDtypeStruct((DEPTH, 8, n_mod), F32),
        grid=(DEPTH, n_mod // tn),
        in_specs=[
            pl.BlockSpec((8, D_MODEL), lambda l, j: (0, 0)),
            pl.BlockSpec((None, D_MODEL, tn), lambda l, j: (l, 0, j)),
            pl.BlockSpec((None, 1, tn), lambda l, j: (l, 0, j)),
        ],
        out_specs=pl.BlockSpec((None, 8, tn), lambda l, j: (l, 0, j)),
        compiler_params=_cparams(2),
        name="modulation",
    )(cond8, w_mod, b_mod.reshape(DEPTH, 1, n_mod))


def _rms(x):
    return x * lax.rsqrt(jnp.mean(x * x, axis=-1, keepdims=True) + NORM_EPS)


def _rope(x, cos, sin):
    lane = lax.broadcasted_iota(jnp.int32, x.shape, 1)
    swapped = jnp.where((lane & 31) < 16, -pltpu.roll(x, LANES - 16, 1), pltpu.roll(x, 16, 1))
    return x * cos + swapped * sin


def _in_kernel(cond_base, cond_stride, latent, x_ref, mod_ref, g_ref, w_ref, *rest):
    if latent:
        cos_ref, sin_ref, z_ref, q_ref, k_ref, v_ref = rest
    else:
        z_ref, q_ref, k_ref, v_ref, kc_ref, vc_ref = rest[-6:]
    row = cond_base + cond_stride * pl.program_id(0)
    shift = mod_ref[pl.ds(row, 1), pl.ds(0, D_MODEL)]
    scale = mod_ref[pl.ds(row, 1), pl.ds(D_MODEL, D_MODEL)]
    h = (_rms(x_ref[...]) * g_ref[...] * (1.0 + scale) + shift).astype(BF16)
    z_ref[:, 0:COL_AQ] = jnp.dot(h, w_ref[:, 0:COL_AQ], preferred_element_type=F32)
    z_ref[:, COL_AQ:Z_W] = jnp.dot(h, w_ref[:, COL_HY:D_IN], preferred_element_type=F32)
    att = jnp.dot(h, w_ref[:, COL_AQ:COL_HY], preferred_element_type=F32)
    q = att[:, 0:DA_W]
    k = att[:, DA_W:2 * DA_W]
    v = att[:, 2 * DA_W:3 * DA_W]
    if latent:
        cos = cos_ref[...]
        sin = sin_ref[...]
        heads = [slice(hh * LANES, (hh + 1) * LANES) for hh in range(DA_HEADS)]
        q = jnp.concatenate([_rope(q[:, s], cos, sin) for s in heads], axis=1)
        k = jnp.concatenate([_rope(k[:, s], cos, sin) for s in heads], axis=1)
    else:
        for hh in range(DA_HEADS):
            for m in range(2):
                c0 = (2 * hh + m) * DA_DQK
                kc_ref[hh, m] = k[:, c0:c0 + DA_DQK].astype(kc_ref.dtype)
            vc_ref[hh] = v[:, hh * DA_DV:(hh + 1) * DA_DV].astype(vc_ref.dtype)
    q_ref[...] = (q * (DA_DQK ** -0.5)).astype(BF16)
    k_ref[...] = k.astype(BF16)
    v_ref[...] = v.astype(BF16)


def _in_proj(x, mod, g, w_bf, cond_base, cond_stride, layer, rope_tabs=None, caches=None, cache_dtype=None):
    B, L, _ = x.shape
    tm = 256
    latent = rope_tabs is not None
    rows = lambda w: pl.BlockSpec((None, tm, w), lambda b, i: (b, i, 0))
    in_specs = [
        rows(D_MODEL),
        pl.BlockSpec((None, 8, N_MOD * D_MODEL), lambda b, i: (layer, 0, 0)),
        pl.BlockSpec((1, D_MODEL), lambda b, i: (0, 0)),
        pl.BlockSpec((None, D_MODEL, D_IN), lambda b, i: (layer, 0, 0)),
    ]
    args = [x, mod, g, w_bf]
    out_shape = [jax.ShapeDtypeStruct((B, L, Z_W), F32)] + [jax.ShapeDtypeStruct((B, L, DA_W), BF16)] * 3
    out_specs = [rows(Z_W), rows(DA_W), rows(DA_W), rows(DA_W)]
    aliases = {}
    if latent:
        in_specs += [pl.BlockSpec((tm, LANES), lambda b, i: (i, 0))] * 2
        args += list(rope_tabs)
    else:
        assert tm == L
        out_shape += [jax.ShapeDtypeStruct((B, DEPTH, DA_HEADS, 2, L, DA_DQK), cache_dtype),
                      jax.ShapeDtypeStruct((B, DEPTH, DA_HEADS, L, DA_DV), cache_dtype)]
        out_specs += [pl.BlockSpec((None, None, DA_HEADS, 2, tm, DA_DQK), lambda b, i: (b, layer, 0, 0, i, 0)),
                      pl.BlockSpec((None, None, DA_HEADS, tm, DA_DV), lambda b, i: (b, layer, 0, i, 0))]
        if caches is not None:
            in_specs += [pl.BlockSpec(memory_space=pl.ANY)] * 2
            args += list(caches)
            aliases = {4: 4, 5: 5}
    return pl.pallas_call(
        functools.partial(_in_kernel, cond_base, cond_stride, latent),
        out_shape=tuple(out_shape),
        grid=(B, L // tm),
        in_specs=in_specs,
        out_specs=tuple(out_specs),
        input_output_aliases=aliases,
        compiler_params=_cparams(2),
        name="in_proj",
    )(*args)


def _head_block_mask(n, width):
    r = lax.broadcasted_iota(jnp.int32, (n, n), 0) // width
    c = lax.broadcasted_iota(jnp.int32, (n, n), 1) // width
    return r == c


def _hgrn_kernel(L, layer, has_state, q_ref, ff_ref, fb_ref, v_ref, g_ref, lbl_ref, ng_ref, *rest):
    if has_state:
        s0_ref = rest[0]
        rest = rest[1:]
    o_ref, sfin_ref, qh_s, vh_s, bf_s, bb_s, kf_s, kb_s, of_s, ob_s, st_s = rest
    C = HGRN_CHUNK
    G = HGRN_GROUP
    R = G * C
    W = HG_W
    HALF = W // 2
    ng_groups = L // R

    lg = [lbl_ref[j] for j in range(DEPTH)]
    mx = functools.reduce(jnp.maximum, lg)
    ex = [jnp.exp(a - mx) for a in lg]
    den = functools.reduce(lambda a, b: a + b, ex)
    sm = [e / den for e in ex]
    cum = sm[0]
    for j in range(1, layer + 1):
        cum = cum + sm[j]
    lb = cum - sm[0]

    rowc = lax.broadcasted_iota(jnp.int32, (L, W), 0) & (C - 1)

    def gates(f, lbd):
        e = jnp.exp(-jnp.abs(f))
        r = 1.0 / (1.0 + e)
        log_sig = jnp.minimum(f, 0.0) + jnp.log(r)
        sig_neg = jnp.where(f >= 0.0, e * r, r)
        if layer == 0:
            return sig_neg, log_sig
        a = jnp.log(lbd)
        c = jnp.log1p(-lbd) + log_sig
        logf = jnp.maximum(a, c) + jnp.log1p(jnp.exp(-jnp.abs(a - c)))
        return (1.0 - lbd) * sig_neg, logf

    kf, lf = gates(ff_ref[...], lb[0:1])
    kb, lbk = gates(fb_ref[...], lb[1:2])

    def put(ref, x):
        ref[0] = x[:, 0:HALF]
        ref[1] = x[:, HALF:W]

    def get(ref, rows):
        return jnp.concatenate([ref.at[0][rows, :], ref.at[1][rows, :]], axis=1)

    put(qh_s, q_ref[...])
    put(vh_s, v_ref[...])
    put(kf_s, kf)
    put(kb_s, kb)
    d = 1
    while d < C:
        lf = lf + jnp.where(rowc >= d, pltpu.roll(lf, d, 0), 0.0)
        lbk = lbk + jnp.where(rowc < C - d, pltpu.roll(lbk, L - d, 0), 0.0)
        d *= 2
    put(bf_s, lf)
    put(bb_s, lbk)

    zero = jnp.zeros((HG_DK, HG_DK), F32)
    for d in range(2):
        for p in range(2):
            if has_state:
                s_a, s_b = s0_ref[d, 2 * p], s0_ref[d, 2 * p + 1]
                blocks = jnp.concatenate([jnp.concatenate([s_a, zero], axis=1),
                                          jnp.concatenate([zero, s_b], axis=1)], axis=0)
                st_s[d, p] = jnp.transpose(blocks)
            else:
                st_s[d, p] = jnp.zeros((HALF, HALF), F32)
    bd_bf = jnp.where(_head_block_mask(W, HG_DK), 1.0, 0.0).astype(BF16)
    bd_half = _head_block_mask(HALF, HG_DK)

    def head_sum(p):
        return jnp.dot(p.astype(BF16), bd_bf, preferred_element_type=F32)

    def group(r0, b_s, k_s, o_s, d, forward):
        def pos(ref, s):
            return get(ref, pl.ds(r0 + s, G, stride=C))

        pairs = [(t, s) for s in range(C) for t in (range(s, C) if forward else range(s + 1))]
        prods = [pos(qh_s, t) * pos(k_s, s) if t == s
                 else jnp.exp(pos(b_s, t) - pos(b_s, s)) * pos(qh_s, t) * pos(k_s, s) for t, s in pairs]
        a = head_sum(jnp.concatenate(prods, axis=0))
        acc = [None] * C
        for i, (t, s) in enumerate(pairs):
            term = a[i * G:(i + 1) * G] * pos(vh_s, s)
            acc[t] = term if acc[t] is None else acc[t] + term
        for t in range(C):
            o_s.at[0][pl.ds(r0 + t, G, stride=C), :] = acc[t][:, 0:HALF]
            o_s.at[1][pl.ds(r0 + t, G, stride=C), :] = acc[t][:, HALF:W]
        order = list(range(G)) if forward else list(range(G - 1, -1, -1))
        halves = [slice(p * HALF, (p + 1) * HALF) for p in range(2)]
        qe, kv, dec = {}, {}, {}
        for c in order:
            rows = pl.ds(r0 + c * C, C)
            b = get(b_s, rows)
            v = get(vh_s, rows)
            b_end = b[C - 1:C] if forward else b[0:1]
            qe[c] = get(qh_s, rows) * jnp.exp(b)
            ke = get(k_s, rows) * jnp.exp(b_end - b)
            dec[c] = jnp.exp(b_end)
            kv[c] = [jnp.where(bd_half, _dot_tn(v[:, ln], ke[:, ln]), 0.0) for ln in halves]
        st = [st_s[d, p] for p in range(2)]
        before = {}
        for c in order:
            before[c] = [s.astype(BF16) for s in st]
            st = [st[p] * dec[c][:, halves[p]] + kv[c][p] for p in range(2)]
        for p in range(2):
            st_s[d, p] = st[p]
        for c in order:
            rows = pl.ds(r0 + c * C, C)
            for p in range(2):
                o_s.at[p][rows, :] = o_s.at[p][rows, :] + _dot_nt(qe[c][:, halves[p]], before[c][p])

    def body(i, carry):
        group(pl.multiple_of(i * R, R), bf_s, kf_s, of_s, 0, True)
        group(pl.multiple_of((ng_groups - 1 - i) * R, R), bb_s, kb_s, ob_s, 1, False)
        return carry

    lax.fori_loop(0, ng_groups, body, 0)

    o = jnp.concatenate([of_s[0] + ob_s[0], of_s[1] + ob_s[1]], axis=1)
    ms = _dot_hi(o * o, jnp.where(_head_block_mask(W, HG_DK), 1.0 / HG_DK, 0.0).astype(F32))
    o_ref[...] = o * lax.rsqrt(ms + NORM_EPS) * ng_ref[...] * _silu(g_ref[...])
    for d in range(2):
        for p in range(2):
            blocks = jnp.transpose(st_s[d, p])
            for hl in range(2):
                sfin_ref[d, 2 * p + hl] = blocks[hl * HG_DK:(hl + 1) * HG_DK, hl * HG_DK:(hl + 1) * HG_DK]


def _hgrn(z, lb_logits, ng_tiled, layer, state=None):
    B, L, _ = z.shape
    W = HG_W
    HALF = W // 2

    def col(c):
        return pl.BlockSpec((None, L, W), lambda b, c=c: (b, 0, c // W))

    seq = pltpu.VMEM((2, L, HALF), F32)
    st_shape = (2, HG_HEADS, HG_DK, HG_DK)
    in_specs = [col(COL_HQ), col(COL_HFF), col(COL_HFB), col(COL_HI), col(COL_HG),
                pl.BlockSpec((DEPTH, 2, W), lambda b: (0, 0, 0)),
                pl.BlockSpec((1, W), lambda b: (0, 0))]
    args = [z, z, z, z, z, lb_logits, ng_tiled]
    if state is not None:
        in_specs.append(pl.BlockSpec((None, None) + st_shape, lambda b: (b, layer, 0, 0, 0, 0)))
        args.append(state)
    return pl.pallas_call(
        functools.partial(_hgrn_kernel, L, layer, state is not None),
        out_shape=(jax.ShapeDtypeStruct((B, L, W), F32),
                   jax.ShapeDtypeStruct((B,) + st_shape, F32)),
        grid=(B,),
        in_specs=in_specs,
        out_specs=(pl.BlockSpec((None, L, W), lambda b: (b, 0, 0)),
                   pl.BlockSpec((None,) + st_shape, lambda b: (b, 0, 0, 0, 0))),
        scratch_shapes=[seq] * 8 + [pltpu.VMEM((2, 2, HALF, HALF), F32)],
        compiler_params=_cparams(1),
        name="hgrn",
    )(*args)


def _attn_kernel(latent, lam_init, n_heads, q_ref, k_ref, v_ref, *rest):
    if latent:
        ck_ref, cv_ref, lam_ref, sg_ref, o_ref = rest
    else:
        lam_ref, sg_ref, o_ref = rest
    lv = lam_ref[...]
    lam = (jnp.exp(jnp.sum(lv[0:1] * lv[1:2], axis=1, keepdims=True))
           - jnp.exp(jnp.sum(lv[2:3] * lv[3:4], axis=1, keepdims=True)) + lam_init)
    for hh in range(n_heads):
        hs = slice(hh * LANES, (hh + 1) * LANES)
        q = q_ref[:, hs]
        k = k_ref[:, hs]
        v = v_ref[:, hs]
        lane = lax.broadcasted_iota(jnp.int32, q.shape, 1)
        outs = []
        for m in range(2):
            qm = jnp.where((lane >= m * DA_DQK) & (lane < (m + 1) * DA_DQK), q, jnp.zeros_like(q))
            s = lax.dot_general(qm, k, (((1,), (1,)), ((), ())), preferred_element_type=F32)
            mx = jnp.max(s, axis=1, keepdims=True)
            if latent:
                sc = _dot_nt(q[:, m * DA_DQK:(m + 1) * DA_DQK], ck_ref[hh, m])
                mx = jnp.maximum(mx, jnp.max(sc, axis=1, keepdims=True))
            p = jnp.exp(s - mx)
            den = jnp.sum(p, axis=1, keepdims=True)
            o = jnp.dot(p.astype(BF16), v, preferred_element_type=F32)
            if latent:
                pc = jnp.exp(sc - mx)
                den = den + jnp.sum(pc, axis=1, keepdims=True)
                o = o + _dot(pc, cv_ref[hh])
            outs.append(o / den)
        o = outs[0] - lam * outs[1]
        o_ref[:, hs] = _rms(o) * sg_ref[...] * (1.0 - lam_init)


def _attention(q, k, v, layer, lam_p, subln_g, ctx_k=None, ctx_v=None):
    B, L, _ = q.shape
    latent = ctx_k is not None
    lam_init = 0.8 - 0.6 * math.exp(-0.3 * layer)
    if latent:
        tq, nh = 256, 4
        grid = (B, DA_HEADS // nh, L // tq)
        P = ctx_k.shape[4]
        in_specs = [
            pl.BlockSpec((None, tq, nh * LANES), lambda b, h, i: (b, i, h)),
            pl.BlockSpec((None, L, nh * LANES), lambda b, h, i: (b, 0, h)),
            pl.BlockSpec((None, L, nh * LANES), lambda b, h, i: (b, 0, h)),
            pl.BlockSpec((None, None, nh, 2, P, DA_DQK), lambda b, h, i: (b, layer, h, 0, 0, 0)),
            pl.BlockSpec((None, None, nh, P, DA_DV), lambda b, h, i: (b, layer, h, 0, 0)),
            pl.BlockSpec((4, DA_DQK), lambda b, h, i: (0, 0)),
            pl.BlockSpec((1, DA_DV), lambda b, h, i: (0, 0)),
        ]
        args = [q, k, v, ctx_k, ctx_v, lam_p, subln_g]
        out_spec = pl.BlockSpec((None, tq, nh * DA_DV), lambda b, h, i: (b, i, h))
    else:
        nh = DA_HEADS
        grid = (B,)
        full = pl.BlockSpec((None, L, DA_W), lambda b: (b, 0, 0))
        in_specs = [full, full, full,
                    pl.BlockSpec((4, DA_DQK), lambda b: (0, 0)),
                    pl.BlockSpec((1, DA_DV), lambda b: (0, 0))]
        args = [q, k, v, lam_p, subln_g]
        out_spec = full
    return pl.pallas_call(
        functools.partial(_attn_kernel, latent, lam_init, nh),
        out_shape=jax.ShapeDtypeStruct((B, L, DA_W), F32),
        grid=grid,
        in_specs=in_specs,
        out_specs=out_spec,
        compiler_params=_cparams(len(grid)),
        name="diff_attention",
    )(*args)


def _rope_tables(L):
    half = DA_DQK // 2
    inv = ROPE_BASE ** (-jnp.arange(0, half, 2, dtype=F32) / half)
    t = jnp.arange(L)
    rows = (t // GRID_W).astype(F32)[:, None] * inv[None, :]
    cols = (t % GRID_W).astype(F32)[:, None] * inv[None, :]
    ang = jnp.concatenate([rows, rows, cols, cols], axis=-1)
    ang = jnp.concatenate([ang, ang], axis=-1)
    return jnp.cos(ang), jnp.sin(ang)


@functools.lru_cache(maxsize=None)
def _dft_np(L):
    idx = (np.arange(L, dtype=np.int64)[:, None] * np.arange(L, dtype=np.int64)[None, :]) % (2 * L)
    ang = idx.astype(np.float64) * (math.pi / L)
    return np.cos(ang).astype(np.float32), np.sin(ang).astype(np.float32)


def _dft_mats(L):
    c, s = _dft_np(L)
    return jnp.asarray(c).astype(BF16), jnp.asarray(s).astype(BF16)


def _filt_hidden_kernel(z_ref, w1_ref, b1_ref, w2_ref, b2_ref, fr_ref, hid_ref):
    fr = fr_ref[...]
    hid = jnp.sin(fr[0:1] * (_dot_hi(z_ref[...], w1_ref[...]) + b1_ref[...]))
    hid_ref[...] = jnp.sin(fr[1:2] * (_dot_hi(hid, w2_ref[...]) + b2_ref[...]))


def _filt_taps_kernel(z_ref, hid_ref, w3f_ref, w3b_ref, dl_ref, sum_ref, diff_ref, nyq_ref):
    hid = hid_ref[...]
    decay = jnp.exp(-z_ref[:, 0:1] * jnp.abs(dl_ref[...]))
    hf = _dot_hi(hid, w3f_ref[...]) * decay
    hb = _dot_hi(hid, w3b_ref[...]) * decay
    nrm = (jnp.sum(jnp.abs(hf), axis=0, keepdims=True)
           + jnp.sum(jnp.abs(hb), axis=0, keepdims=True))
    row = lax.broadcasted_iota(jnp.int32, hf.shape, 0)
    hf = hf / nrm
    hb = jnp.where(row == 0, 0.0, hb / nrm)
    fs = hf + hb
    sum_ref[...] = fs.astype(BF16)
    diff_ref[...] = (hf - hb).astype(BF16)
    sgn = jnp.where((row & 1) == 0, 1.0, -1.0)
    nyq_ref[...] = jnp.broadcast_to(jnp.sum(fs * sgn, axis=0, keepdims=True), nyq_ref.shape)


def _filt_spec_kernel(c_ref, s_ref, sum_ref, diff_ref, nyq_ref, a_ref, b_ref, d_ref):
    a = _dot(c_ref[...], sum_ref[...])
    him = -_dot(s_ref[...], diff_ref[...])
    row = lax.broadcasted_iota(jnp.int32, a.shape, 0) + pl.program_id(0) * a.shape[0]
    a_ref[...] = a
    b_ref[...] = jnp.where(row == 0, 0.0, him)
    d_ref[...] = jnp.where(row == 0, nyq_ref[0:1], a)


def _hyena_filters(L, w1, b1, w2, b2, w3, freq):
    t = jnp.linspace(0.0, 1.0, L, dtype=F32)[:, None]
    bands = (HY_POS_EMB - 1) // 2
    fr = jnp.linspace(1e-4, bands - 1, bands, dtype=F32)[None, :]
    w = 2.0 * math.pi * jnp.arange(L, dtype=F32)[:, None] / L
    z = jnp.concatenate([t, jnp.cos(fr * w), -jnp.sin(fr * w)], axis=-1)
    z = jnp.pad(z, ((0, 0), (0, HY_HIDDEN - HY_POS_EMB)))
    w1p = jnp.pad(w1, ((0, 0), (0, HY_HIDDEN - HY_POS_EMB), (0, 0)))
    max_decay = math.log(HY_TARGET) / HY_SHORT_DECAY_PCT
    min_decay = math.log(HY_TARGET) / HY_LONG_DECAY_PCT
    deltas = jnp.linspace(min_decay, max_decay, HY_W, dtype=F32)[None, :]
    tc = LANES
    nct = HY_W // tc
    nw = HY_ORDER * HY_W
    hid = pl.pallas_call(
        _filt_hidden_kernel,
        out_shape=jax.ShapeDtypeStruct((DEPTH, L, HY_HIDDEN), F32),
        grid=(DEPTH,),
        in_specs=[
            pl.BlockSpec((L, HY_HIDDEN), lambda l: (0, 0)),
            pl.BlockSpec((None, HY_HIDDEN, HY_HIDDEN), lambda l: (l, 0, 0)),
            pl.BlockSpec((None, 1, HY_HIDDEN), lambda l: (l, 0, 0)),
            pl.BlockSpec((None, HY_HIDDEN, HY_HIDDEN), lambda l: (l, 0, 0)),
            pl.BlockSpec((None, 1, HY_HIDDEN), lambda l: (l, 0, 0)),
            pl.BlockSpec((None, 2, HY_HIDDEN), lambda l: (l, 0, 0)),
        ],
        out_specs=pl.BlockSpec((None, L, HY_HIDDEN), lambda l: (l, 0, 0)),
        compiler_params=_cparams(1),
        name="hyena_filter_hidden",
    )(z, w1p, b1.reshape(DEPTH, 1, HY_HIDDEN), w2, b2.reshape(DEPTH, 1, HY_HIDDEN), freq)
    fsum, fdiff, nyq = pl.pallas_call(
        _filt_taps_kernel,
        out_shape=(jax.ShapeDtypeStruct((DEPTH, L, nw), BF16),
                   jax.ShapeDtypeStruct((DEPTH, L, nw), BF16),
                   jax.ShapeDtypeStruct((DEPTH, 8, nw), F32)),
        grid=(DEPTH, HY_ORDER, nct),
        in_specs=[
            pl.BlockSpec((L, HY_HIDDEN), lambda l, o, j: (0, 0)),
            pl.BlockSpec((None, L, HY_HIDDEN), lambda l, o, j: (l, 0, 0)),
            pl.BlockSpec((None, HY_HIDDEN, tc), lambda l, o, j: (l, 0, o * nct + j)),
            pl.BlockSpec((None, HY_HIDDEN, tc), lambda l, o, j: (l, 0, (HY_ORDER + o) * nct + j)),
            pl.BlockSpec((1, tc), lambda l, o, j: (0, j)),
        ],
        out_specs=(pl.BlockSpec((None, L, tc), lambda l, o, j: (l, 0, o * nct + j)),
                   pl.BlockSpec((None, L, tc), lambda l, o, j: (l, 0, o * nct + j)),
                   pl.BlockSpec((None, 8, tc), lambda l, o, j: (l, 0, o * nct + j))),
        compiler_params=_cparams(3),
        name="hyena_filter_taps",
    )(z, hid, w3, w3, deltas)

    cm, sm = _dft_mats(L)
    tf = min(L, 512)
    spec = jax.ShapeDtypeStruct((DEPTH, L, nw), F32)
    return pl.pallas_call(
        _filt_spec_kernel,
        out_shape=(spec, spec, spec),
        grid=(L // tf, DEPTH),
        in_specs=[
            pl.BlockSpec((tf, L), lambda i, l: (i, 0)),
            pl.BlockSpec((tf, L), lambda i, l: (i, 0)),
            pl.BlockSpec((None, L, nw), lambda i, l: (l, 0, 0)),
            pl.BlockSpec((None, L, nw), lambda i, l: (l, 0, 0)),
            pl.BlockSpec((None, 8, nw), lambda i, l: (l, 0, 0)),
        ],
        out_specs=(pl.BlockSpec((None, tf, nw), lambda i, l: (l, i, 0)),) * 3,
        compiler_params=_cparams(2),
        name="hyena_filter_spectrum",
    )(cm, sm, fsum, fdiff, nyq)


def _short_conv_kernel(L, za_ref, zb_ref, zc_ref, w_ref, b_ref, o_ref):
    W = HY_W
    for r in range(za_ref.shape[0]):
        for j, z_ref in enumerate((za_ref, zb_ref, zc_ref)):
            z = z_ref[r]
            w = w_ref[:, j * W:(j + 1) * W]
            row = lax.broadcasted_iota(jnp.int32, z.shape, 0)
            prev = jnp.where(row == 0, 0.0, pltpu.roll(z, 1, 0))
            nxt = jnp.where(row == L - 1, 0.0, pltpu.roll(z, L - 1, 0))
            o_ref[r, :, j * W:(j + 1) * W] = (b_ref[:, j * W:(j + 1) * W] + prev * w[0:1] + z * w[1:2]
                                              + nxt * w[2:3])


def _short_conv(z, conv_w, conv_b):
    B, L, _ = z.shape
    W = HY_W
    nb = (HY_ORDER + 1)
    cb = Z_HY // W
    rb = _requests_per_step(B, L)
    return pl.pallas_call(
        functools.partial(_short_conv_kernel, L),
        out_shape=jax.ShapeDtypeStruct((B, L, nb * W), F32),
        grid=(B // rb,),
        in_specs=[pl.BlockSpec((rb, L, W), lambda b, j=j: (b, 0, cb + j)) for j in range(nb)]
        + [pl.BlockSpec((3, nb * W), lambda b: (0, 0)),
           pl.BlockSpec((1, nb * W), lambda b: (0, 0))],
        out_specs=pl.BlockSpec((rb, L, nb * W), lambda b: (b, 0, 0)),
        compiler_params=_cparams(1),
        name="hyena_short_conv",
    )(z, z, z, conv_w, conv_b.reshape(1, nb * W))


def _conv_fwd_kernel(u_ref, c_ref, s_ref, a_ref, b_ref, d_ref, yre_ref, yim_ref):
    a = a_ref[...]
    bm = b_ref[...]
    dm = d_ref[...]
    cm = c_ref[...]
    sm = s_ref[...]
    for r in range(u_ref.shape[0]):
        u = u_ref[r]
        ure = _dot(cm, u)
        uim = -_dot(sm, u)
        rowu = lax.broadcasted_iota(jnp.int32, u.shape, 0)
        nyq = jnp.sum(jnp.where((rowu & 1) == 0, u, -u), axis=0, keepdims=True)
        row = lax.broadcasted_iota(jnp.int32, ure.shape, 0) + pl.program_id(0) * ure.shape[0]
        uim = jnp.where(row == 0, nyq, uim)
        yre_ref[r] = (ure * a - uim * bm).astype(BF16)
        yim_ref[r] = (ure * bm + uim * dm).astype(BF16)


def _conv_inv_kernel(L, yre_ref, yim_ref, c_ref, s_ref, u_ref, skip_ref, m_ref, o_ref):
    cm = c_ref[...]
    sm = s_ref[...]
    for r in range(u_ref.shape[0]):
        yre = yre_ref[r]
        yim = yim_ref[r]
        y = (jnp.dot(cm, yre, preferred_element_type=F32)
             - jnp.dot(sm, yim, preferred_element_type=F32)) * (1.0 / L)
        row = lax.broadcasted_iota(jnp.int32, y.shape, 0) + pl.program_id(0) * y.shape[0]
        sgn = jnp.where((row & 1) == 0, 1.0, -1.0)
        y = y + (sgn * yim[0:1].astype(F32) - yre[0:1].astype(F32)) * (0.5 / L)
        o_ref[r] = m_ref[r] * (y + u_ref[r] * skip_ref[...])


def _long_conv(zc, u_col, m_col, filt, layer, order, skip, u_arr=None):
    B, L, _ = zc.shape
    W = HY_W
    cm, sm = _dft_mats(L)
    fa, fb, fd = filt
    tf = min(L, 512)
    rb = _requests_per_step(B, L, CONV_STEP_ROWS, L // tf)
    if u_arr is None:
        u_arr, ub = zc, u_col
    else:
        ub = 0
    spec_t = pl.BlockSpec((tf, L), lambda i, b: (i, 0))
    fspec = pl.BlockSpec((None, tf, W), lambda i, b: (layer, i, order))
    yre, yim = pl.pallas_call(
        _conv_fwd_kernel,
        out_shape=(jax.ShapeDtypeStruct((B, L, W), BF16),) * 2,
        grid=(L // tf, B // rb),
        in_specs=[pl.BlockSpec((rb, L, W), lambda i, b: (b, 0, ub)),
                  spec_t, spec_t, fspec, fspec, fspec],
        out_specs=(pl.BlockSpec((rb, tf, W), lambda i, b: (b, i, 0)),) * 2,
        compiler_params=_cparams(2),
        name="hyena_conv_fwd",
    )(u_arr, cm, sm, fa, fb, fd)
    full = pl.BlockSpec((rb, L, W), lambda i, b: (b, 0, 0))
    return pl.pallas_call(
        functools.partial(_conv_inv_kernel, L),
        out_shape=jax.ShapeDtypeStruct((B, L, W), F32),
        grid=(L // tf, B // rb),
        in_specs=[full, full, spec_t, spec_t,
                  pl.BlockSpec((rb, tf, W), lambda i, b: (b, i, ub)),
                  pl.BlockSpec((None, 1, W), lambda i, b: (order, 0, 0)),
                  pl.BlockSpec((rb, tf, W), lambda i, b: (b, i, m_col))],
        out_specs=pl.BlockSpec((rb, tf, W), lambda i, b: (b, i, 0)),
        compiler_params=_cparams(2),
        name="hyena_conv_inv",
    )(yre, yim, cm, sm, u_arr, skip.reshape(HY_ORDER, 1, W), zc)


def _hyena(z, filt, layer, conv_w, conv_b, skip):
    zc = _short_conv(z, conv_w, conv_b)
    y1 = _long_conv(zc, 0, 1, filt, layer, 0, skip)
    return _long_conv(zc, 0, 2, filt, layer, 1, skip, u_arr=y1)


def _out_kernel(cond_base, cond_stride, hg_ref, da_ref, hy_ref, x_ref, mod_ref, g_ref, w_ref, wr_ref,
                x1_ref, h2_ref, aff_ref):
    row = cond_base + cond_stride * pl.program_id(0)
    D = D_MODEL
    gate1 = mod_ref[pl.ds(row, 1), pl.ds(2 * D, D)]
    shift2 = mod_ref[pl.ds(row, 1), pl.ds(3 * D, D)]
    scale2 = mod_ref[pl.ds(row, 1), pl.ds(4 * D, D)]
    wr = wr_ref[...]
    w_hi = wr.astype(BF16)
    w_lo = (wr - w_hi.astype(F32)).astype(BF16)
    for r0 in range(0, x_ref.shape[0], OUT_SUB_ROWS):
        rs = slice(r0, r0 + OUT_SUB_ROWS)
        mix = (_dot(hg_ref[rs, :], w_ref[0:HG_W, :])
               + _dot(da_ref[rs, :], w_ref[HG_W:HG_W + DA_W, :])
               + _dot(hy_ref[rs, :], w_ref[HG_W + DA_W:D, :]))
        x1 = x_ref[rs, :] + gate1 * mix
        x1_ref[rs, :] = x1
        h2 = _rms(x1) * g_ref[...] * (1.0 + scale2) + shift2
        h_hi = h2.astype(BF16)
        h2_ref[rs, :] = h_hi
        h_lo = (h2 - h_hi.astype(F32)).astype(BF16)
        logits = (jnp.dot(h_hi, w_hi, preferred_element_type=F32)
                  + jnp.dot(h_hi, w_lo, preferred_element_type=F32)
                  + jnp.dot(h_lo, w_hi, preferred_element_type=F32))
        lane = lax.broadcasted_iota(jnp.int32, logits.shape, 1)
        logits = jnp.where(lane < N_EXPERTS, logits, -jnp.inf)
        e = jnp.exp(logits - jnp.max(logits, axis=1, keepdims=True))
        aff_ref[rs, :] = e / jnp.sum(e, axis=1, keepdims=True)


def _out_proj(o_hg, o_da, o_hy, x, mod, g, w_bf, wr_pad, cond_base, cond_stride, layer):
    B, L, D = x.shape
    tm = min(L, 512)

    def rows(w):
        return pl.BlockSpec((None, tm, w), lambda b, i: (b, i, 0))

    return pl.pallas_call(
        functools.partial(_out_kernel, cond_base, cond_stride),
        out_shape=(jax.ShapeDtypeStruct((B, L, D), F32),
                   jax.ShapeDtypeStruct((B, L, D), BF16),
                   jax.ShapeDtypeStruct((B, L, LANES), F32)),
        grid=(B, L // tm),
        in_specs=[rows(HG_W), rows(DA_W), rows(HY_W), rows(D),
                  pl.BlockSpec((None, 8, N_MOD * D), lambda b, i: (layer, 0, 0)),
                  pl.BlockSpec((1, D), lambda b, i: (0, 0)),
                  pl.BlockSpec((None, D, D), lambda b, i: (layer, 0, 0)),
                  pl.BlockSpec((None, D, LANES), lambda b, i: (layer, 0, 0))],
        out_specs=(rows(D), rows(D), rows(LANES)),
        compiler_params=_cparams(2),
        name="out_proj",
    )(o_hg, o_da, o_hy, x, mod, g, w_bf, wr_pad)


def _prefix_count(m, blk):
    L = m.shape[0]
    r = lax.broadcasted_iota(jnp.int32, (blk, blk), 0)
    c = lax.broadcasted_iota(jnp.int32, (blk, blk), 1)
    tri = jnp.where(c < r, 1.0, 0.0).astype(BF16)
    carry = jnp.zeros((1, m.shape[1]), F32)
    parts = []
    for i in range(L // blk):
        mb = m[i * blk:(i + 1) * blk]
        parts.append(jnp.dot(tri, mb.astype(BF16), preferred_element_type=F32) + carry)
        carry = carry + jnp.sum(mb, axis=0, keepdims=True)
    return jnp.concatenate(parts, axis=0) if len(parts) > 1 else parts[0]


def _route_kernel(cap, aff_ref, slot_ref, slot_t_ref):
    for r in range(aff_ref.shape[0]):
        _route_one(cap, aff_ref.at[r], slot_ref.at[r], slot_t_ref.at[r])


def _route_one(cap, aff_ref, slot_ref, slot_t_ref):
    aff = aff_ref[...]
    capf = float(cap)
    tiny = 2.0 ** -126

    aff_t = jnp.transpose(aff)[0:N_EXPERTS, :]

    def enough(t):
        return jnp.sum(jnp.where(aff_t >= t, 1.0, 0.0), axis=1, keepdims=True) >= capf

    has = enough(jnp.full((N_EXPERTS, 1), tiny, F32))
    lo = jnp.full((N_EXPERTS, 1), tiny, F32)
    for s in (64, 32, 16, 8, 4, 2, 1):
        cand = lo * (2.0 ** s)
        lo = jnp.where(enough(cand), cand, lo)
    step = lo
    for _ in range(23):
        step = step * 0.5
        cand = lo + step
        lo = jnp.where(enough(cand), cand, lo)
    hi = jnp.where(has, lo + jnp.maximum(step, tiny), tiny)
    lo = jnp.where(has, lo, 0.0)

    def to_row(col):
        r = lax.broadcasted_iota(jnp.int32, (N_EXPERTS, LANES), 0)
        c = lax.broadcasted_iota(jnp.int32, (N_EXPERTS, LANES), 1)
        return jnp.sum(jnp.where(r == c, col, 0.0), axis=0, keepdims=True)

    hi = to_row(hi)
    lo = to_row(lo)
    gt = jnp.where(aff >= hi, 1.0, 0.0)
    eq = jnp.where((aff >= lo) & (aff < hi), 1.0, 0.0)
    need = capf - jnp.sum(gt, axis=0, keepdims=True)
    blk = min(aff.shape[0], 256)
    sel = gt + eq * jnp.where(_prefix_count(eq, blk) < need, 1.0, 0.0)
    lane = lax.broadcasted_iota(jnp.int32, aff.shape, 1)
    sel = jnp.where(lane < N_EXPERTS, sel, 0.0)
    slot = jnp.where(sel > 0.0, _prefix_count(sel, blk), -1.0)
    slot_ref[...] = slot
    slot_t_ref[...] = jnp.transpose(slot)[0:N_EXPERTS, :]


def _route(aff, cap):
    B, L, _ = aff.shape
    rb = _requests_per_step(B, L)
    return pl.pallas_call(
        functools.partial(_route_kernel, cap),
        out_shape=(jax.ShapeDtypeStruct((B, L, LANES), F32),
                   jax.ShapeDtypeStruct((B, N_EXPERTS, L), F32)),
        grid=(B // rb,),
        in_specs=[pl.BlockSpec((rb, L, LANES), lambda b: (b, 0, 0))],
        out_specs=(pl.BlockSpec((rb, L, LANES), lambda b: (b, 0, 0)),
                   pl.BlockSpec((rb, N_EXPERTS, L), lambda b: (b, 0, 0))),
        compiler_params=_cparams(1),
        name="route",
    )(aff)


def _gather_kernel(cap, eg, slot_t_ref, h_ref, xs_ref):
    e0 = pl.program_id(1) * eg
    c = lax.broadcasted_iota(jnp.int32, (cap, slot_t_ref.shape[1]), 0).astype(F32)
    onehot = jnp.concatenate(
        [jnp.where(c == slot_t_ref[pl.ds(e0 + j, 1), :], 1.0, 0.0) for j in range(eg)],
        axis=0).astype(BF16)
    xs = jnp.dot(onehot, h_ref[...], preferred_element_type=F32).astype(BF16)
    for j in range(eg):
        xs_ref[j] = xs[j * cap:(j + 1) * cap]


def _gather(slot_t, h2, cap):
    B, L, D = h2.shape
    eg = max(1, min(N_EXPERTS, GATHER_ROWS // cap))
    return pl.pallas_call(
        functools.partial(_gather_kernel, cap, eg),
        out_shape=jax.ShapeDtypeStruct((N_EXPERTS, B * cap, D), BF16),
        grid=(B, N_EXPERTS // eg),
        in_specs=[pl.BlockSpec((None, N_EXPERTS, L), lambda b, e: (b, 0, 0)),
                  pl.BlockSpec((None, L, D), lambda b, e: (b, 0, 0))],
        out_specs=pl.BlockSpec((eg, cap, D), lambda b, e: (e, b, 0)),
        compiler_params=_cparams(2),
        name="expert_gather",
    )(slot_t, h2)


def _ffn_kernel(xa_ref, xb_ref, wg_ref, wu_ref, wd_ref, oa_ref, ob_ref):
    wg = wg_ref[...].astype(BF16)
    wu = wu_ref[...].astype(BF16)
    wd = wd_ref[...].astype(BF16)
    for x_ref, o_ref in ((xa_ref, oa_ref), (xb_ref, ob_ref)):
        x = x_ref[...]
        hid = (_silu(jnp.dot(x, wg, preferred_element_type=F32))
               * jnp.dot(x, wu, preferred_element_type=F32))
        o_ref[...] = jnp.dot(hid.astype(BF16), wd, preferred_element_type=F32).astype(BF16)


def _expert_ffn(xs_a, xs_b, w_gate, w_up, w_down, layer):
    E, S, D = xs_a.shape
    xspec = pl.BlockSpec((None, S, D), lambda e: (e, 0, 0))
    out = jax.ShapeDtypeStruct((E, S, D), BF16)
    return pl.pallas_call(
        _ffn_kernel,
        out_shape=(out, out),
        grid=(E,),
        in_specs=[xspec, xspec,
                  pl.BlockSpec((None, None, D, D_FF), lambda e: (layer, e, 0, 0)),
                  pl.BlockSpec((None, None, D, D_FF), lambda e: (layer, e, 0, 0)),
                  pl.BlockSpec((None, None, D_FF, D), lambda e: (layer, e, 0, 0))],
        out_specs=(xspec, xspec),
        compiler_params=_cparams(1),
        name="expert_ffn",
    )(xs_a, xs_b, w_gate, w_up, w_down)


def _scatter_kernel(cap, eg, cond_base, cond_stride, final, ys_ref, slot_ref, aff_ref, x_ref, mod_ref, fg_ref,
                    o_ref):
    row = cond_base + cond_stride * pl.program_id(0)
    gate2 = mod_ref[pl.ds(row, 1), pl.ds(5 * D_MODEL, D_MODEL)]
    slot = slot_ref[...].astype(BF16)
    aff = aff_ref[...].astype(BF16)
    tt = slot.shape[0]
    K = eg * cap
    shift = cap.bit_length() - 1
    col_expert = lax.broadcasted_iota(jnp.int32, (LANES, K), 1) >> shift
    lane_expert = lax.broadcasted_iota(jnp.int32, (LANES, K), 0)
    col_slot = (lax.broadcasted_iota(jnp.int32, (tt, K), 1) & (cap - 1)).astype(F32)
    y = jnp.zeros((tt, D_MODEL), F32)
    for g in range(N_EXPERTS // eg):
        expand = jnp.where(lane_expert == col_expert + g * eg, 1.0, 0.0).astype(BF16)
        slot_x = jnp.dot(slot, expand, preferred_element_type=F32)
        aff_x = jnp.dot(aff, expand, preferred_element_type=F32)
        w = jnp.where(slot_x == col_slot, aff_x, 0.0).astype(BF16)
        ys = jnp.concatenate([ys_ref[g * eg + j] for j in range(eg)], axis=0)
        y = y + jnp.dot(w, ys, preferred_element_type=F32)
    x2 = x_ref[...] + gate2 * y
    if final:
        x2 = _rms(x2) * fg_ref[...]
    o_ref[...] = x2


def _scatter(ys, slot, aff, x1, mod, fg, cap, cond_base, cond_stride, final, layer):
    B, L, D = x1.shape
    tt = min(L, 1024)
    assert cap & (cap - 1) == 0
    eg = max(1, min(N_EXPERTS, GATHER_ROWS // cap))

    def rows(w):
        return pl.BlockSpec((None, tt, w), lambda b, i: (b, i, 0))

    return pl.pallas_call(
        functools.partial(_scatter_kernel, cap, eg, cond_base, cond_stride, final),
        out_shape=jax.ShapeDtypeStruct((B, L, D), F32),
        grid=(B, L // tt),
        in_specs=[pl.BlockSpec((N_EXPERTS, cap, D), lambda b, i: (0, b, 0)),
                  rows(LANES), rows(LANES), rows(D),
                  pl.BlockSpec((None, 8, N_MOD * D), lambda b, i: (layer, 0, 0)),
                  pl.BlockSpec((1, D), lambda b, i: (0, 0))],
        out_specs=rows(D),
        compiler_params=_cparams(2),
        name="expert_scatter",
    )(ys, slot, aff, x1, mod, fg)


def kernel(x_prompt, x_sample, c, cache_diff_k, cache_diff_v, state_hgrn, c_ctx, norm_g, final_norm_g,
           w_mod, b_mod, w_in, w_out, hgrn_lb_logits, hgrn_norm_g, diff_lambda, diff_subln_g,
           hy_conv_w, hy_conv_b, hy_filt_w1, hy_filt_b1, hy_filt_w2, hy_filt_b2, hy_filt_w3, hy_filt_freq,
           hy_skip, w_router, w_gate, w_up, w_down):
    B_p, L_p, D = x_prompt.shape
    B_s, L_s, _ = x_sample.shape
    cap_p = EC_CAPACITY_FACTOR * L_p // N_EXPERTS
    cap_s = EC_CAPACITY_FACTOR * L_s // N_EXPERTS
    assert B_p * cap_p == B_s * cap_s

    cond8 = jnp.zeros((8, D), F32).at[0].set(c_ctx.astype(F32)).at[1:1 + B_s].set(c.astype(F32))
    mod = _modulation(cond8, w_mod, b_mod)
    groups = (
        dict(cond=(0, 0), cap=cap_p, L=L_p),
        dict(cond=(1, 1), cap=cap_s, L=L_s),
    )
    filt = {L: _hyena_filters(L, hy_filt_w1, hy_filt_b1, hy_filt_w2, hy_filt_b2, hy_filt_w3, hy_filt_freq)
            for L in {L_p, L_s}}
    rope_tabs = _rope_tables(L_s)
    fg = final_norm_g.astype(F32).reshape(1, D)
    wr_pad = jnp.pad(w_router.astype(F32), ((0, 0), (0, 0), (0, LANES - N_EXPERTS)))
    caches = None
    w_in_bf = w_in.astype(BF16)
    w_out_bf = w_out.astype(BF16)

    xs = [x_prompt.astype(F32), x_sample.astype(F32)]
    new_s = []
    for l in range(DEPTH):
        ng = jnp.tile(hgrn_norm_g[l].astype(F32), HG_HEADS).reshape(1, HG_W)
        sg = diff_subln_g[l].astype(F32).reshape(1, DA_DV)
        g1 = norm_g[l, 0].astype(F32).reshape(1, D)
        g2 = norm_g[l, 1].astype(F32).reshape(1, D)
        x1s, slots, affs, gathered = [], [], [], []
        for gi, grp in enumerate(groups):
            x = xs[gi]
            cb, cs = grp["cond"]
            if gi == 0:
                z, qa, ka, va, kc, vc = _in_proj(x, mod, g1, w_in_bf, cb, cs, l, caches=caches,
                                                 cache_dtype=x_prompt.dtype)
                caches = (kc, vc)
                o_hg, s_fin = _hgrn(z, hgrn_lb_logits.astype(F32), ng, l)
                new_s.append(s_fin)
            else:
                z, qa, ka, va = _in_proj(x, mod, g1, w_in_bf, cb, cs, l, rope_tabs=rope_tabs)
                o_hg, _ = _hgrn(z, hgrn_lb_logits.astype(F32), ng, l, state=state_hgrn.astype(F32))
            if gi == 0:
                o_da = _attention(qa, ka, va, l, diff_lambda[l].astype(F32), sg)
            else:
                o_da = _attention(qa, ka, va, l, diff_lambda[l].astype(F32), sg,
                                  cache_diff_k.astype(F32), cache_diff_v.astype(F32))
            o_hy = _hyena(z, filt[grp["L"]], l, hy_conv_w[l].astype(F32), hy_conv_b[l].astype(F32),
                          hy_skip[l].astype(F32))
            x1, h2, aff = _out_proj(o_hg, o_da, o_hy, x, mod, g2, w_out_bf, wr_pad, cb, cs, l)
            slot, slot_t = _route(aff, grp["cap"])
            x1s.append(x1)
            slots.append(slot)
            affs.append(aff)
            gathered.append(_gather(slot_t, h2, grp["cap"]))
        ys = _expert_ffn(gathered[0], gathered[1], w_gate, w_up, w_down, l)
        for gi, grp in enumerate(groups):
            cb, cs = grp["cond"]
            xs[gi] = _scatter(ys[gi], slots[gi], affs[gi], x1s[gi], mod, fg, grp["cap"], cb, cs,
                              final=(l == DEPTH - 1), layer=l)

    dt = x_prompt.dtype
    return (xs[0].astype(dt), xs[1].astype(x_sample.dtype),
            caches[0], caches[1], jnp.stack(new_s, axis=1).astype(dt))
```

```python
import functools
import math

import jax
import jax.numpy as jnp
import numpy as np
from jax import lax
from jax.experimental import pallas as pl
from jax.experimental.pallas import tpu as pltpu

F32 = jnp.float32
BF16 = jnp.bfloat16
HI = lax.Precision.HIGHEST

D_MODEL = 1024
DEPTH = 4
GRID_W = 64
HG_HEADS = 4
HG_DK = 64
HG_W = 256
DA_HEADS = 4
DA_DQK = 64
DA_DV = 128
DA_W = 512
HY_W = 256
HY_ORDER = 2
HY_POS_EMB = 33
HY_HIDDEN = 64
HY_SHORT_DECAY_PCT = 0.3
HY_LONG_DECAY_PCT = 1.5
HY_TARGET = 1e-2
N_EXPERTS = 16
EC_CAPACITY_FACTOR = 2
D_FF = 1024
N_MOD = 6
ROPE_BASE = 10000.0
NORM_EPS = 1e-6
D_IN = 3584
COL_HQ, COL_HFF, COL_HFB, COL_HI, COL_HG = 0, 256, 512, 768, 1024
COL_AQ, COL_AK, COL_AV, COL_HY = 1280, 1792, 2304, 2816
Z_W = COL_AQ + (D_IN - COL_HY)
Z_HY = COL_AQ

LANES = 128
HGRN_CHUNK = 16
HGRN_GROUP = 8
GATHER_ROWS = 1024
SHORT_SEQ_ROWS = 1024
CONV_STEP_ROWS = 4096
OUT_SUB_ROWS = 128
VMEM_LIMIT = 56 * 1024 * 1024


def _cparams(n_axes):
    return pltpu.CompilerParams(dimension_semantics=("arbitrary",) * n_axes,
                                vmem_limit_bytes=VMEM_LIMIT)


def _requests_per_step(B, L, rows=None, other_steps=1):
    rb = max(1, min(B, (rows or SHORT_SEQ_ROWS) // L))
    while B % rb or (rb > 1 and other_steps * (B // rb) < 2):
        rb -= 1
    return rb


def _dot(a, b):
    return jnp.dot(a.astype(BF16), b.astype(BF16), preferred_element_type=F32)


def _dot_nt(a, b):
    return lax.dot_general(a.astype(BF16), b.astype(BF16), (((1,), (1,)), ((), ())),
                           preferred_element_type=F32)


def _dot_tn(a, b):
    return lax.dot_general(a.astype(BF16), b.astype(BF16), (((0,), (0,)), ((), ())),
                           preferred_element_type=F32)


def _dot_hi(a, b):
    return jnp.dot(a, b, precision=HI, preferred_element_type=F32)


def _sigmoid(x):
    return 1.0 / (1.0 + jnp.exp(-x))


def _silu(x):
    return x * _sigmoid(x)


def _mod_kernel(cond_ref, w_ref, b_ref, o_ref):
    s = _silu(cond_ref[...])
    o_ref[...] = _dot(s, w_ref[...]) + b_ref[...]


def _modulation(cond8, w_mod, b_mod):
    tn = 1536
    n_mod = N_MOD * D_MODEL
    return pl.pallas_call(
        _mod_kernel,
        out_shape=jax.ShapeDtypeStruct((DEPTH, 8, n_mod), F32),
        grid=(DEPTH, n_mod // tn),
        in_specs=[
            pl.BlockSpec((8, D_MODEL), lambda l, j: (0, 0)),
            pl.BlockSpec((None, D_MODEL, tn), lambda l, j: (l, 0, j)),
            pl.BlockSpec((None, 1, tn), lambda l, j: (l, 0, j)),
        ],
        out_specs=pl.BlockSpec((None, 8, tn), lambda l, j: (l, 0, j)),
        compiler_params=_cparams(2),
        name="modulation",
    )(cond8, w_mod, b_mod.reshape(DEPTH, 1, n_mod))


def _rms(x):
    return x * lax.rsqrt(jnp.mean(x * x, axis=-1, keepdims=True) + NORM_EPS)


def _rope(x, cos, sin):
    lane = lax.broadcasted_iota(jnp.int32, x.shape, 1)
    swapped = jnp.where((lane & 31) < 16, -pltpu.roll(x, LANES - 16, 1), pltpu.roll(x, 16, 1))
    return x * cos + swapped * sin


def _in_kernel(cond_base, cond_stride, latent, x_ref, mod_ref, g_ref, w_ref, *rest):
    if latent:
        cos_ref, sin_ref, z_ref, q_ref, k_ref, v_ref = rest
    else:
        z_ref, q_ref, k_ref, v_ref, kc_ref, vc_ref = rest[-6:]
    row = cond_base + cond_stride * pl.program_id(0)
    shift = mod_ref[pl.ds(row, 1), pl.ds(0, D_MODEL)]
    scale = mod_ref[pl.ds(row, 1), pl.ds(D_MODEL, D_MODEL)]
    h = (_rms(x_ref[...]) * g_ref[...] * (1.0 + scale) + shift).astype(BF16)
    z_ref[:, 0:COL_AQ] = jnp.dot(h, w_ref[:, 0:COL_AQ], preferred_element_type=F32)
    z_ref[:, COL_AQ:Z_W] = jnp.dot(h, w_ref[:, COL_HY:D_IN], preferred_element_type=F32)
    att = jnp.dot(h, w_ref[:, COL_AQ:COL_HY], preferred_element_type=F32)
    q = att[:, 0:DA_W]
    k = att[:, DA_W:2 * DA_W]
    v = att[:, 2 * DA_W:3 * DA_W]
    if latent:
        cos = cos_ref[...]
        sin = sin_ref[...]
        heads = [slice(hh * LANES, (hh + 1) * LANES) for hh in range(DA_HEADS)]
        q = jnp.concatenate([_rope(q[:, s], cos, sin) for s in heads], axis=1)
        k = jnp.concatenate([_rope(k[:, s], cos, sin) for s in heads], axis=1)
    else:
        for hh in range(DA_HEADS):
            for m in range(2):
                c0 = (2 * hh + m) * DA_DQK
                kc_ref[hh, m] = k[:, c0:c0 + DA_DQK].astype(kc_ref.dtype)
            vc_ref[hh] = v[:, hh * DA_DV:(hh + 1) * DA_DV].astype(vc_ref.dtype)
    q_ref[...] = (q * (DA_DQK ** -0.5)).astype(BF16)
    k_ref[...] = k.astype(BF16)
    v_ref[...] = v.astype(BF16)


def _in_proj(x, mod, g, w_bf, cond_base, cond_stride, layer, rope_tabs=None, caches=None, cache_dtype=None):
    B, L, _ = x.shape
    tm = 256
    latent = rope_tabs is not None
    rows = lambda w: pl.BlockSpec((None, tm, w), lambda b, i: (b, i, 0))
    in_specs = [
        rows(D_MODEL),
        pl.BlockSpec((None, 8, N_MOD * D_MODEL), lambda b, i: (layer, 0, 0)),
        pl.BlockSpec((1, D_MODEL), lambda b, i: (0, 0)),
        pl.BlockSpec((None, D_MODEL, D_IN), lambda b, i: (layer, 0, 0)),
    ]
    args = [x, mod, g, w_bf]
    out_shape = [jax.ShapeDtypeStruct((B, L, Z_W), F32)] + [jax.ShapeDtypeStruct((B, L, DA_W), BF16)] * 3
    out_specs = [rows(Z_W), rows(DA_W), rows(DA_W), rows(DA_W)]
    aliases = {}
    if latent:
        in_specs += [pl.BlockSpec((tm, LANES), lambda b, i: (i, 0))] * 2
        args += list(rope_tabs)
    else:
        assert tm == L
        out_shape += [jax.ShapeDtypeStruct((B, DEPTH, DA_HEADS, 2, L, DA_DQK), cache_dtype),
                      jax.ShapeDtypeStruct((B, DEPTH, DA_HEADS, L, DA_DV), cache_dtype)]
        out_specs += [pl.BlockSpec((None, None, DA_HEADS, 2, tm, DA_DQK), lambda b, i: (b, layer, 0, 0, i, 0)),
                      pl.BlockSpec((None, None, DA_HEADS, tm, DA_DV), lambda b, i: (b, layer, 0, i, 0))]
        if caches is not None:
            in_specs += [pl.BlockSpec(memory_space=pl.ANY)] * 2
            args += list(caches)
            aliases = {4: 4, 5: 5}
    return pl.pallas_call(
        functools.partial(_in_kernel, cond_base, cond_stride, latent),
        out_shape=tuple(out_shape),
        grid=(B, L // tm),
        in_specs=in_specs,
        out_specs=tuple(out_specs),
        input_output_aliases=aliases,
        compiler_params=_cparams(2),
        name="in_proj",
    )(*args)


def _head_block_mask(n, width):
    r = lax.broadcasted_iota(jnp.int32, (n, n), 0) // width
    c = lax.broadcasted_iota(jnp.int32, (n, n), 1) // width
    return r == c


def _hgrn_kernel(L, layer, has_state, q_ref, ff_ref, fb_ref, v_ref, g_ref, lbl_ref, ng_ref, *rest):
    if has_state:
        s0_ref = rest[0]
        rest = rest[1:]
    o_ref, sfin_ref, qh_s, vh_s, bf_s, bb_s, kf_s, kb_s, of_s, ob_s, st_s = rest
    C = HGRN_CHUNK
    G = HGRN_GROUP
    R = G * C
    W = HG_W
    HALF = W // 2
    ng_groups = L // R

    lg = [lbl_ref[j] for j in range(DEPTH)]
    mx = functools.reduce(jnp.maximum, lg)
    ex = [jnp.exp(a - mx) for a in lg]
    den = functools.reduce(lambda a, b: a + b, ex)
    sm = [e / den for e in ex]
    cum = sm[0]
    for j in range(1, layer + 1):
        cum = cum + sm[j]
    lb = cum - sm[0]

    rowc = lax.broadcasted_iota(jnp.int32, (L, W), 0) & (C - 1)

    def gates(f, lbd):
        e = jnp.exp(-jnp.abs(f))
        r = 1.0 / (1.0 + e)
        log_sig = jnp.minimum(f, 0.0) + jnp.log(r)
        sig_neg = jnp.where(f >= 0.0, e * r, r)
        if layer == 0:
            return sig_neg, log_sig
        a = jnp.log(lbd)
        c = jnp.log1p(-lbd) + log_sig
        logf = jnp.maximum(a, c) + jnp.log1p(jnp.exp(-jnp.abs(a - c)))
        return (1.0 - lbd) * sig_neg, logf

    kf, lf = gates(ff_ref[...], lb[0:1])
    kb, lbk = gates(fb_ref[...], lb[1:2])

    def put(ref, x):
        ref[0] = x[:, 0:HALF]
        ref[1] = x[:, HALF:W]

    def get(ref, rows):
        return jnp.concatenate([ref.at[0][rows, :], ref.at[1][rows, :]], axis=1)

    put(qh_s, q_ref[...])
    put(vh_s, v_ref[...])
    put(kf_s, kf)
    put(kb_s, kb)
    d = 1
    while d < C:
        lf = lf + jnp.where(rowc >= d, pltpu.roll(lf, d, 0), 0.0)
        lbk = lbk + jnp.where(rowc < C - d, pltpu.roll(lbk, L - d, 0), 0.0)
        d *= 2
    put(bf_s, lf)
    put(bb_s, lbk)

    zero = jnp.zeros((HG_DK, HG_DK), F32)
    for d in range(2):
        for p in range(2):
            if has_state:
                s_a, s_b = s0_ref[d, 2 * p], s0_ref[d, 2 * p + 1]
                blocks = jnp.concatenate([jnp.concatenate([s_a, zero], axis=1),
                                          jnp.concatenate([zero, s_b], axis=1)], axis=0)
                st_s[d, p] = jnp.transpose(blocks)
            else:
                st_s[d, p] = jnp.zeros((HALF, HALF), F32)
    bd_bf = jnp.where(_head_block_mask(W, HG_DK), 1.0, 0.0).astype(BF16)
    bd_half = _head_block_mask(HALF, HG_DK)

    def head_sum(p):
        return jnp.dot(p.astype(BF16), bd_bf, preferred_element_type=F32)

    def group(r0, b_s, k_s, o_s, d, forward):
        def pos(ref, s):
            return get(ref, pl.ds(r0 + s, G, stride=C))

        pairs = [(t, s) for s in range(C) for t in (range(s, C) if forward else range(s + 1))]
        prods = [pos(qh_s, t) * pos(k_s, s) if t == s
                 else jnp.exp(pos(b_s, t) - pos(b_s, s)) * pos(qh_s, t) * pos(k_s, s) for t, s in pairs]
        a = head_sum(jnp.concatenate(prods, axis=0))
        acc = [None] * C
        for i, (t, s) in enumerate(pairs):
            term = a[i * G:(i + 1) * G] * pos(vh_s, s)
            acc[t] = term if acc[t] is None else acc[t] + term
        for t in range(C):
            o_s.at[0][pl.ds(r0 + t, G, stride=C), :] = acc[t][:, 0:HALF]
            o_s.at[1][pl.ds(r0 + t, G, stride=C), :] = acc[t][:, HALF:W]
        order = list(range(G)) if forward else list(range(G - 1, -1, -1))
        halves = [slice(p * HALF, (p + 1) * HALF) for p in range(2)]
        qe, kv, dec = {}, {}, {}
        for c in order:
            rows = pl.ds(r0 + c * C, C)
            b = get(b_s, rows)
            v = get(vh_s, rows)
            b_end = b[C - 1:C] if forward else b[0:1]
            qe[c] = get(qh_s, rows) * jnp.exp(b)
            ke = get(k_s, rows) * jnp.exp(b_end - b)
            dec[c] = jnp.exp(b_end)
            kv[c] = [jnp.where(bd_half, _dot_tn(v[:, ln], ke[:, ln]), 0.0) for ln in halves]
        st = [st_s[d, p] for p in range(2)]
        before = {}
        for c in order:
            before[c] = [s.astype(BF16) for s in st]
            st = [st[p] * dec[c][:, halves[p]] + kv[c][p] for p in range(2)]
        for p in range(2):
            st_s[d, p] = st[p]
        for c in order:
            rows = pl.ds(r0 + c * C, C)
            for p in range(2):
                o_s.at[p][rows, :] = o_s.at[p][rows, :] + _dot_nt(qe[c][:, halves[p]], before[c][p])

    def body(i, carry):
        group(pl.multiple_of(i * R, R), bf_s, kf_s, of_s, 0, True)
        group(pl.multiple_of((ng_groups - 1 - i) * R, R), bb_s, kb_s, ob_s, 1, False)
        return carry

    lax.fori_loop(0, ng_groups, body, 0)

    o = jnp.concatenate([of_s[0] + ob_s[0], of_s[1] + ob_s[1]], axis=1)
    ms = _dot_hi(o * o, jnp.where(_head_block_mask(W, HG_DK), 1.0 / HG_DK, 0.0).astype(F32))
    o_ref[...] = (o * lax.rsqrt(ms + NORM_EPS) * ng_ref[...] * _silu(g_ref[...])).astype(o_ref.dtype)
    for d in range(2):
        for p in range(2):
            blocks = jnp.transpose(st_s[d, p])
            for hl in range(2):
                sfin_ref[d, 2 * p + hl] = blocks[hl * HG_DK:(hl + 1) * HG_DK, hl * HG_DK:(hl + 1) * HG_DK]


def _hgrn(z, lb_logits, ng_tiled, layer, state=None):
    B, L, _ = z.shape
    W = HG_W
    HALF = W // 2

    def col(c):
        return pl.BlockSpec((None, L, W), lambda b, c=c: (b, 0, c // W))

    seq = pltpu.VMEM((2, L, HALF), F32)
    st_shape = (2, HG_HEADS, HG_DK, HG_DK)
    in_specs = [col(COL_HQ), col(COL_HFF), col(COL_HFB), col(COL_HI), col(COL_HG),
                pl.BlockSpec((DEPTH, 2, W), lambda b: (0, 0, 0)),
                pl.BlockSpec((1, W), lambda b: (0, 0))]
    args = [z, z, z, z, z, lb_logits, ng_tiled]
    if state is not None:
        in_specs.append(pl.BlockSpec((None, None) + st_shape, lambda b: (b, layer, 0, 0, 0, 0)))
        args.append(state)
    return pl.pallas_call(
        functools.partial(_hgrn_kernel, L, layer, state is not None),
        out_shape=(jax.ShapeDtypeStruct((B, L, W), BF16),
                   jax.ShapeDtypeStruct((B,) + st_shape, F32)),
        grid=(B,),
        in_specs=in_specs,
        out_specs=(pl.BlockSpec((None, L, W), lambda b: (b, 0, 0)),
                   pl.BlockSpec((None,) + st_shape, lambda b: (b, 0, 0, 0, 0))),
        scratch_shapes=[seq] * 8 + [pltpu.VMEM((2, 2, HALF, HALF), F32)],
        compiler_params=_cparams(1),
        name="hgrn",
    )(*args)


def _attn_kernel(latent, lam_init, n_heads, q_ref, k_ref, v_ref, *rest):
    if latent:
        ck_ref, cv_ref, lam_ref, sg_ref, o_ref = rest
    else:
        lam_ref, sg_ref, o_ref = rest
    lv = lam_ref[...]
    lam = (jnp.exp(jnp.sum(lv[0:1] * lv[1:2], axis=1, keepdims=True))
           - jnp.exp(jnp.sum(lv[2:3] * lv[3:4], axis=1, keepdims=True)) + lam_init)
    for hh in range(n_heads):
        hs = slice(hh * LANES, (hh + 1) * LANES)
        q = q_ref[:, hs]
        k = k_ref[:, hs]
        v = v_ref[:, hs]
        lane = lax.broadcasted_iota(jnp.int32, q.shape, 1)
        outs = []
        for m in range(2):
            qm = jnp.where((lane >= m * DA_DQK) & (lane < (m + 1) * DA_DQK), q, jnp.zeros_like(q))
            s = lax.dot_general(qm, k, (((1,), (1,)), ((), ())), preferred_element_type=F32)
            mx = jnp.max(s, axis=1, keepdims=True)
            if latent:
                sc = _dot_nt(q[:, m * DA_DQK:(m + 1) * DA_DQK], ck_ref[hh, m])
                mx = jnp.maximum(mx, jnp.max(sc, axis=1, keepdims=True))
            p = jnp.exp(s - mx)
            den = jnp.sum(p, axis=1, keepdims=True)
            o = jnp.dot(p.astype(BF16), v, preferred_element_type=F32)
            if latent:
                pc = jnp.exp(sc - mx)
                den = den + jnp.sum(pc, axis=1, keepdims=True)
                o = o + _dot(pc, cv_ref[hh])
            outs.append(o / den)
        o = outs[0] - lam * outs[1]
        o_ref[:, hs] = (_rms(o) * sg_ref[...] * (1.0 - lam_init)).astype(o_ref.dtype)


def _attention(q, k, v, layer, lam_p, subln_g, ctx_k=None, ctx_v=None):
    B, L, _ = q.shape
    latent = ctx_k is not None
    lam_init = 0.8 - 0.6 * math.exp(-0.3 * layer)
    if latent:
        tq, nh = 256, 4
        grid = (B, DA_HEADS // nh, L // tq)
        P = ctx_k.shape[4]
        in_specs = [
            pl.BlockSpec((None, tq, nh * LANES), lambda b, h, i: (b, i, h)),
            pl.BlockSpec((None, L, nh * LANES), lambda b, h, i: (b, 0, h)),
            pl.BlockSpec((None, L, nh * LANES), lambda b, h, i: (b, 0, h)),
            pl.BlockSpec((None, None, nh, 2, P, DA_DQK), lambda b, h, i: (b, layer, h, 0, 0, 0)),
            pl.BlockSpec((None, None, nh, P, DA_DV), lambda b, h, i: (b, layer, h, 0, 0)),
            pl.BlockSpec((4, DA_DQK), lambda b, h, i: (0, 0)),
            pl.BlockSpec((1, DA_DV), lambda b, h, i: (0, 0)),
        ]
        args = [q, k, v, ctx_k, ctx_v, lam_p, subln_g]
        out_spec = pl.BlockSpec((None, tq, nh * DA_DV), lambda b, h, i: (b, i, h))
    else:
        nh = DA_HEADS
        grid = (B,)
        full = pl.BlockSpec((None, L, DA_W), lambda b: (b, 0, 0))
        in_specs = [full, full, full,
                    pl.BlockSpec((4, DA_DQK), lambda b: (0, 0)),
                    pl.BlockSpec((1, DA_DV), lambda b: (0, 0))]
        args = [q, k, v, lam_p, subln_g]
        out_spec = full
    return pl.pallas_call(
        functools.partial(_attn_kernel, latent, lam_init, nh),
        out_shape=jax.ShapeDtypeStruct((B, L, DA_W), BF16),
        grid=grid,
        in_specs=in_specs,
        out_specs=out_spec,
        compiler_params=_cparams(len(grid)),
        name="diff_attention",
    )(*args)


def _rope_tables(L):
    half = DA_DQK // 2
    inv = ROPE_BASE ** (-jnp.arange(0, half, 2, dtype=F32) / half)
    t = jnp.arange(L)
    rows = (t // GRID_W).astype(F32)[:, None] * inv[None, :]
    cols = (t % GRID_W).astype(F32)[:, None] * inv[None, :]
    ang = jnp.concatenate([rows, rows, cols, cols], axis=-1)
    ang = jnp.concatenate([ang, ang], axis=-1)
    return jnp.cos(ang), jnp.sin(ang)


@functools.lru_cache(maxsize=None)
def _dft_np(L):
    idx = (np.arange(L, dtype=np.int64)[:, None] * np.arange(L, dtype=np.int64)[None, :]) % (2 * L)
    ang = idx.astype(np.float64) * (math.pi / L)
    return np.cos(ang).astype(np.float32), np.sin(ang).astype(np.float32)


def _dft_mats(L):
    c, s = _dft_np(L)
    return jnp.asarray(c).astype(BF16), jnp.asarray(s).astype(BF16)


def _filt_hidden_kernel(z_ref, w1_ref, b1_ref, w2_ref, b2_ref, fr_ref, hid_ref):
    fr = fr_ref[...]
    hid = jnp.sin(fr[0:1] * (_dot_hi(z_ref[...], w1_ref[...]) + b1_ref[...]))
    hid_ref[...] = jnp.sin(fr[1:2] * (_dot_hi(hid, w2_ref[...]) + b2_ref[...]))


def _filt_taps_kernel(z_ref, hid_ref, w3f_ref, w3b_ref, dl_ref, sum_ref, diff_ref, nyq_ref):
    hid = hid_ref[...]
    decay = jnp.exp(-z_ref[:, 0:1] * jnp.abs(dl_ref[...]))
    hf = _dot_hi(hid, w3f_ref[...]) * decay
    hb = _dot_hi(hid, w3b_ref[...]) * decay
    nrm = (jnp.sum(jnp.abs(hf), axis=0, keepdims=True)
           + jnp.sum(jnp.abs(hb), axis=0, keepdims=True))
    row = lax.broadcasted_iota(jnp.int32, hf.shape, 0)
    hf = hf / nrm
    hb = jnp.where(row == 0, 0.0, hb / nrm)
    fs = hf + hb
    sum_ref[...] = fs.astype(BF16)
    diff_ref[...] = (hf - hb).astype(BF16)
    sgn = jnp.where((row & 1) == 0, 1.0, -1.0)
    nyq_ref[...] = jnp.broadcast_to(jnp.sum(fs * sgn, axis=0, keepdims=True), nyq_ref.shape)


def _filt_spec_kernel(c_ref, s_ref, sum_ref, diff_ref, nyq_ref, a_ref, b_ref, d_ref):
    a = _dot(c_ref[...], sum_ref[...])
    him = -_dot(s_ref[...], diff_ref[...])
    row = lax.broadcasted_iota(jnp.int32, a.shape, 0) + pl.program_id(0) * a.shape[0]
    a_ref[...] = a
    b_ref[...] = jnp.where(row == 0, 0.0, him)
    d_ref[...] = jnp.where(row == 0, nyq_ref[0:1], a)


def _hyena_filters(L, w1, b1, w2, b2, w3, freq):
    t = jnp.linspace(0.0, 1.0, L, dtype=F32)[:, None]
    bands = (HY_POS_EMB - 1) // 2
    fr = jnp.linspace(1e-4, bands - 1, bands, dtype=F32)[None, :]
    w = 2.0 * math.pi * jnp.arange(L, dtype=F32)[:, None] / L
    z = jnp.concatenate([t, jnp.cos(fr * w), -jnp.sin(fr * w)], axis=-1)
    z = jnp.pad(z, ((0, 0), (0, HY_HIDDEN - HY_POS_EMB)))
    w1p = jnp.pad(w1, ((0, 0), (0, HY_HIDDEN - HY_POS_EMB), (0, 0)))
    max_decay = math.log(HY_TARGET) / HY_SHORT_DECAY_PCT
    min_decay = math.log(HY_TARGET) / HY_LONG_DECAY_PCT
    deltas = jnp.linspace(min_decay, max_decay, HY_W, dtype=F32)[None, :]
    tc = LANES
    nct = HY_W // tc
    nw = HY_ORDER * HY_W
    hid = pl.pallas_call(
        _filt_hidden_kernel,
        out_shape=jax.ShapeDtypeStruct((DEPTH, L, HY_HIDDEN), F32),
        grid=(DEPTH,),
        in_specs=[
            pl.BlockSpec((L, HY_HIDDEN), lambda l: (0, 0)),
            pl.BlockSpec((None, HY_HIDDEN, HY_HIDDEN), lambda l: (l, 0, 0)),
            pl.BlockSpec((None, 1, HY_HIDDEN), lambda l: (l, 0, 0)),
            pl.BlockSpec((None, HY_HIDDEN, HY_HIDDEN), lambda l: (l, 0, 0)),
            pl.BlockSpec((None, 1, HY_HIDDEN), lambda l: (l, 0, 0)),
            pl.BlockSpec((None, 2, HY_HIDDEN), lambda l: (l, 0, 0)),
        ],
        out_specs=pl.BlockSpec((None, L, HY_HIDDEN), lambda l: (l, 0, 0)),
        compiler_params=_cparams(1),
        name="hyena_filter_hidden",
    )(z, w1p, b1.reshape(DEPTH, 1, HY_HIDDEN), w2, b2.reshape(DEPTH, 1, HY_HIDDEN), freq)
    fsum, fdiff, nyq = pl.pallas_call(
        _filt_taps_kernel,
        out_shape=(jax.ShapeDtypeStruct((DEPTH, L, nw), BF16),
                   jax.ShapeDtypeStruct((DEPTH, L, nw), BF16),
                   jax.ShapeDtypeStruct((DEPTH, 8, nw), F32)),
        grid=(DEPTH, HY_ORDER, nct),
        in_specs=[
            pl.BlockSpec((L, HY_HIDDEN), lambda l, o, j: (0, 0)),
            pl.BlockSpec((None, L, HY_HIDDEN), lambda l, o, j: (l, 0, 0)),
            pl.BlockSpec((None, HY_HIDDEN, tc), lambda l, o, j: (l, 0, o * nct + j)),
            pl.BlockSpec((None, HY_HIDDEN, tc), lambda l, o, j: (l, 0, (HY_ORDER + o) * nct + j)),
            pl.BlockSpec((1, tc), lambda l, o, j: (0, j)),
        ],
        out_specs=(pl.BlockSpec((None, L, tc), lambda l, o, j: (l, 0, o * nct + j)),
                   pl.BlockSpec((None, L, tc), lambda l, o, j: (l, 0, o * nct + j)),
                   pl.BlockSpec((None, 8, tc), lambda l, o, j: (l, 0, o * nct + j))),
        compiler_params=_cparams(3),
        name="hyena_filter_taps",
    )(z, hid, w3, w3, deltas)

    cm, sm = _dft_mats(L)
    tf = min(L, 512)
    spec = jax.ShapeDtypeStruct((DEPTH, L, nw), F32)
    return pl.pallas_call(
        _filt_spec_kernel,
        out_shape=(spec, spec, spec),
        grid=(L // tf, DEPTH),
        in_specs=[
            pl.BlockSpec((tf, L), lambda i, l: (i, 0)),
            pl.BlockSpec((tf, L), lambda i, l: (i, 0)),
            pl.BlockSpec((None, L, nw), lambda i, l: (l, 0, 0)),
            pl.BlockSpec((None, L, nw), lambda i, l: (l, 0, 0)),
            pl.BlockSpec((None, 8, nw), lambda i, l: (l, 0, 0)),
        ],
        out_specs=(pl.BlockSpec((None, tf, nw), lambda i, l: (l, i, 0)),) * 3,
        compiler_params=_cparams(2),
        name="hyena_filter_spectrum",
    )(cm, sm, fsum, fdiff, nyq)


def _short_conv_kernel(L, za_ref, zb_ref, zc_ref, w_ref, b_ref, o_ref):
    W = HY_W
    for r in range(za_ref.shape[0]):
        for j, z_ref in enumerate((za_ref, zb_ref, zc_ref)):
            z = z_ref[r]
            w = w_ref[:, j * W:(j + 1) * W]
            row = lax.broadcasted_iota(jnp.int32, z.shape, 0)
            prev = jnp.where(row == 0, 0.0, pltpu.roll(z, 1, 0))
            nxt = jnp.where(row == L - 1, 0.0, pltpu.roll(z, L - 1, 0))
            o_ref[r, :, j * W:(j + 1) * W] = (b_ref[:, j * W:(j + 1) * W] + prev * w[0:1] + z * w[1:2]
                                              + nxt * w[2:3])


def _short_conv(z, conv_w, conv_b):
    B, L, _ = z.shape
    W = HY_W
    nb = (HY_ORDER + 1)
    cb = Z_HY // W
    rb = _requests_per_step(B, L)
    return pl.pallas_call(
        functools.partial(_short_conv_kernel, L),
        out_shape=jax.ShapeDtypeStruct((B, L, nb * W), F32),
        grid=(B // rb,),
        in_specs=[pl.BlockSpec((rb, L, W), lambda b, j=j: (b, 0, cb + j)) for j in range(nb)]
        + [pl.BlockSpec((3, nb * W), lambda b: (0, 0)),
           pl.BlockSpec((1, nb * W), lambda b: (0, 0))],
        out_specs=pl.BlockSpec((rb, L, nb * W), lambda b: (b, 0, 0)),
        compiler_params=_cparams(1),
        name="hyena_short_conv",
    )(z, z, z, conv_w, conv_b.reshape(1, nb * W))


def _conv_fwd_kernel(u_ref, c_ref, s_ref, a_ref, b_ref, d_ref, yre_ref, yim_ref):
    a = a_ref[...]
    bm = b_ref[...]
    dm = d_ref[...]
    cm = c_ref[...]
    sm = s_ref[...]
    for r in range(u_ref.shape[0]):
        u = u_ref[r]
        ure = _dot(cm, u)
        uim = -_dot(sm, u)
        rowu = lax.broadcasted_iota(jnp.int32, u.shape, 0)
        nyq = jnp.sum(jnp.where((rowu & 1) == 0, u, -u), axis=0, keepdims=True)
        row = lax.broadcasted_iota(jnp.int32, ure.shape, 0) + pl.program_id(0) * ure.shape[0]
        uim = jnp.where(row == 0, nyq, uim)
        yre_ref[r] = (ure * a - uim * bm).astype(BF16)
        yim_ref[r] = (ure * bm + uim * dm).astype(BF16)


def _conv_inv_kernel(L, yre_ref, yim_ref, c_ref, s_ref, u_ref, skip_ref, m_ref, o_ref):
    cm = c_ref[...]
    sm = s_ref[...]
    for r in range(u_ref.shape[0]):
        yre = yre_ref[r]
        yim = yim_ref[r]
        y = (jnp.dot(cm, yre, preferred_element_type=F32)
             - jnp.dot(sm, yim, preferred_element_type=F32)) * (1.0 / L)
        row = lax.broadcasted_iota(jnp.int32, y.shape, 0) + pl.program_id(0) * y.shape[0]
        sgn = jnp.where((row & 1) == 0, 1.0, -1.0)
        y = y + (sgn * yim[0:1].astype(F32) - yre[0:1].astype(F32)) * (0.5 / L)
        o_ref[r] = (m_ref[r] * (y + u_ref[r] * skip_ref[...])).astype(o_ref.dtype)


def _long_conv(zc, u_col, m_col, filt, layer, order, skip, u_arr=None, out_dtype=F32):
    B, L, _ = zc.shape
    W = HY_W
    cm, sm = _dft_mats(L)
    fa, fb, fd = filt
    tf = min(L, 512)
    rb = _requests_per_step(B, L, CONV_STEP_ROWS, L // tf)
    if u_arr is None:
        u_arr, ub = zc, u_col
    else:
        ub = 0
    spec_t = pl.BlockSpec((tf, L), lambda i, b: (i, 0))
    fspec = pl.BlockSpec((None, tf, W), lambda i, b: (layer, i, order))
    yre, yim = pl.pallas_call(
        _conv_fwd_kernel,
        out_shape=(jax.ShapeDtypeStruct((B, L, W), BF16),) * 2,
        grid=(L // tf, B // rb),
        in_specs=[pl.BlockSpec((rb, L, W), lambda i, b: (b, 0, ub)),
                  spec_t, spec_t, fspec, fspec, fspec],
        out_specs=(pl.BlockSpec((rb, tf, W), lambda i, b: (b, i, 0)),) * 2,
        compiler_params=_cparams(2),
        name="hyena_conv_fwd",
    )(u_arr, cm, sm, fa, fb, fd)
    full = pl.BlockSpec((rb, L, W), lambda i, b: (b, 0, 0))
    return pl.pallas_call(
        functools.partial(_conv_inv_kernel, L),
        out_shape=jax.ShapeDtypeStruct((B, L, W), out_dtype),
        grid=(L // tf, B // rb),
        in_specs=[full, full, spec_t, spec_t,
                  pl.BlockSpec((rb, tf, W), lambda i, b: (b, i, ub)),
                  pl.BlockSpec((None, 1, W), lambda i, b: (order, 0, 0)),
                  pl.BlockSpec((rb, tf, W), lambda i, b: (b, i, m_col))],
        out_specs=pl.BlockSpec((rb, tf, W), lambda i, b: (b, i, 0)),
        compiler_params=_cparams(2),
        name="hyena_conv_inv",
    )(yre, yim, cm, sm, u_arr, skip.reshape(HY_ORDER, 1, W), zc)


def _hyena(z, filt, layer, conv_w, conv_b, skip):
    zc = _short_conv(z, conv_w, conv_b)
    y1 = _long_conv(zc, 0, 1, filt, layer, 0, skip)
    return _long_conv(zc, 0, 2, filt, layer, 1, skip, u_arr=y1, out_dtype=BF16)


def _out_kernel(cond_base, cond_stride, hg_ref, da_ref, hy_ref, x_ref, mod_ref, g_ref, w_ref, wr_ref,
                x1_ref, h2_ref, aff_ref):
    row = cond_base + cond_stride * pl.program_id(0)
    D = D_MODEL
    gate1 = mod_ref[pl.ds(row, 1), pl.ds(2 * D, D)]
    shift2 = mod_ref[pl.ds(row, 1), pl.ds(3 * D, D)]
    scale2 = mod_ref[pl.ds(row, 1), pl.ds(4 * D, D)]
    wr = wr_ref[...]
    w_hi = wr.astype(BF16)
    w_lo = (wr - w_hi.astype(F32)).astype(BF16)
    for r0 in range(0, x_ref.shape[0], OUT_SUB_ROWS):
        rs = slice(r0, r0 + OUT_SUB_ROWS)
        mix = (_dot(hg_ref[rs, :], w_ref[0:HG_W, :])
               + _dot(da_ref[rs, :], w_ref[HG_W:HG_W + DA_W, :])
               + _dot(hy_ref[rs, :], w_ref[HG_W + DA_W:D, :]))
        x1 = x_ref[rs, :] + gate1 * mix
        x1_ref[rs, :] = x1
        h2 = _rms(x1) * g_ref[...] * (1.0 + scale2) + shift2
        h_hi = h2.astype(BF16)
        h2_ref[rs, :] = h_hi
        h_lo = (h2 - h_hi.astype(F32)).astype(BF16)
        logits = (jnp.dot(h_hi, w_hi, preferred_element_type=F32)
                  + jnp.dot(h_hi, w_lo, preferred_element_type=F32)
                  + jnp.dot(h_lo, w_hi, preferred_element_type=F32))
        lane = lax.broadcasted_iota(jnp.int32, logits.shape, 1)
        logits = jnp.where(lane < N_EXPERTS, logits, -jnp.inf)
        e = jnp.exp(logits - jnp.max(logits, axis=1, keepdims=True))
        aff_ref[rs, :] = e / jnp.sum(e, axis=1, keepdims=True)


def _out_proj(o_hg, o_da, o_hy, x, mod, g, w_bf, wr_pad, cond_base, cond_stride, layer):
    B, L, D = x.shape
    tm = min(L, 512)

    def rows(w):
        return pl.BlockSpec((None, tm, w), lambda b, i: (b, i, 0))

    return pl.pallas_call(
        functools.partial(_out_kernel, cond_base, cond_stride),
        out_shape=(jax.ShapeDtypeStruct((B, L, D), F32),
                   jax.ShapeDtypeStruct((B, L, D), BF16),
                   jax.ShapeDtypeStruct((B, L, LANES), F32)),
        grid=(B, L // tm),
        in_specs=[rows(HG_W), rows(DA_W), rows(HY_W), rows(D),
                  pl.BlockSpec((None, 8, N_MOD * D), lambda b, i: (layer, 0, 0)),
                  pl.BlockSpec((1, D), lambda b, i: (0, 0)),
                  pl.BlockSpec((None, D, D), lambda b, i: (layer, 0, 0)),
                  pl.BlockSpec((None, D, LANES), lambda b, i: (layer, 0, 0))],
        out_specs=(rows(D), rows(D), rows(LANES)),
        compiler_params=_cparams(2),
        name="out_proj",
    )(o_hg, o_da, o_hy, x, mod, g, w_bf, wr_pad)


def _prefix_count(m, blk):
    L = m.shape[0]
    r = lax.broadcasted_iota(jnp.int32, (blk, blk), 0)
    c = lax.broadcasted_iota(jnp.int32, (blk, blk), 1)
    tri = jnp.where(c < r, 1.0, 0.0).astype(BF16)
    carry = jnp.zeros((1, m.shape[1]), F32)
    parts = []
    for i in range(L // blk):
        mb = m[i * blk:(i + 1) * blk]
        parts.append(jnp.dot(tri, mb.astype(BF16), preferred_element_type=F32) + carry)
        carry = carry + jnp.sum(mb, axis=0, keepdims=True)
    return jnp.concatenate(parts, axis=0) if len(parts) > 1 else parts[0]


def _route_kernel(cap, aff_ref, slot_ref, slot_t_ref):
    for r in range(aff_ref.shape[0]):
        _route_one(cap, aff_ref.at[r], slot_ref.at[r], slot_t_ref.at[r])


def _route_one(cap, aff_ref, slot_ref, slot_t_ref):
    aff = aff_ref[...]
    capf = float(cap)
    tiny = 2.0 ** -126

    aff_t = jnp.transpose(aff)[0:N_EXPERTS, :]

    def enough(t):
        return jnp.sum(jnp.where(aff_t >= t, 1.0, 0.0), axis=1, keepdims=True) >= capf

    has = enough(jnp.full((N_EXPERTS, 1), tiny, F32))
    lo = jnp.full((N_EXPERTS, 1), tiny, F32)
    for s in (64, 32, 16, 8, 4, 2, 1):
        cand = lo * (2.0 ** s)
        lo = jnp.where(enough(cand), cand, lo)
    step = lo
    for _ in range(23):
        step = step * 0.5
        cand = lo + step
        lo = jnp.where(enough(cand), cand, lo)
    hi = jnp.where(has, lo + jnp.maximum(step, tiny), tiny)
    lo = jnp.where(has, lo, 0.0)

    def to_row(col):
        r = lax.broadcasted_iota(jnp.int32, (N_EXPERTS, LANES), 0)
        c = lax.broadcasted_iota(jnp.int32, (N_EXPERTS, LANES), 1)
        return jnp.sum(jnp.where(r == c, col, 0.0), axis=0, keepdims=True)

    hi = to_row(hi)
    lo = to_row(lo)
    gt = jnp.where(aff >= hi, 1.0, 0.0)
    eq = jnp.where((aff >= lo) & (aff < hi), 1.0, 0.0)
    need = capf - jnp.sum(gt, axis=0, keepdims=True)
    blk = min(aff.shape[0], 256)
    sel = gt + eq * jnp.where(_prefix_count(eq, blk) < need, 1.0, 0.0)
    lane = lax.broadcasted_iota(jnp.int32, aff.shape, 1)
    sel = jnp.where(lane < N_EXPERTS, sel, 0.0)
    slot = jnp.where(sel > 0.0, _prefix_count(sel, blk), -1.0)
    slot_ref[...] = slot
    slot_t_ref[...] = jnp.transpose(slot)[0:N_EXPERTS, :]


def _route(aff, cap):
    B, L, _ = aff.shape
    rb = _requests_per_step(B, L)
    return pl.pallas_call(
        functools.partial(_route_kernel, cap),
        out_shape=(jax.ShapeDtypeStruct((B, L, LANES), F32),
                   jax.ShapeDtypeStruct((B, N_EXPERTS, L), F32)),
        grid=(B // rb,),
        in_specs=[pl.BlockSpec((rb, L, LANES), lambda b: (b, 0, 0))],
        out_specs=(pl.BlockSpec((rb, L, LANES), lambda b: (b, 0, 0)),
                   pl.BlockSpec((rb, N_EXPERTS, L), lambda b: (b, 0, 0))),
        compiler_params=_cparams(1),
        name="route",
    )(aff)


def _gather_kernel(cap, eg, slot_t_ref, h_ref, xs_ref):
    e0 = pl.program_id(1) * eg
    c = lax.broadcasted_iota(jnp.int32, (cap, slot_t_ref.shape[1]), 0).astype(F32)
    onehot = jnp.concatenate(
        [jnp.where(c == slot_t_ref[pl.ds(e0 + j, 1), :], 1.0, 0.0) for j in range(eg)],
        axis=0).astype(BF16)
    xs = jnp.dot(onehot, h_ref[...], preferred_element_type=F32).astype(BF16)
    for j in range(eg):
        xs_ref[j] = xs[j * cap:(j + 1) * cap]


def _gather(slot_t, h2, cap):
    B, L, D = h2.shape
    eg = max(1, min(N_EXPERTS, GATHER_ROWS // cap))
    return pl.pallas_call(
        functools.partial(_gather_kernel, cap, eg),
        out_shape=jax.ShapeDtypeStruct((N_EXPERTS, B * cap, D), BF16),
        grid=(B, N_EXPERTS // eg),
        in_specs=[pl.BlockSpec((None, N_EXPERTS, L), lambda b, e: (b, 0, 0)),
                  pl.BlockSpec((None, L, D), lambda b, e: (b, 0, 0))],
        out_specs=pl.BlockSpec((eg, cap, D), lambda b, e: (e, b, 0)),
        compiler_params=_cparams(2),
        name="expert_gather",
    )(slot_t, h2)


def _ffn_kernel(xa_ref, xb_ref, wg_ref, wu_ref, wd_ref, oa_ref, ob_ref):
    wg = wg_ref[...].astype(BF16)
    wu = wu_ref[...].astype(BF16)
    wd = wd_ref[...].astype(BF16)
    for x_ref, o_ref in ((xa_ref, oa_ref), (xb_ref, ob_ref)):
        x = x_ref[...]
        hid = (_silu(jnp.dot(x, wg, preferred_element_type=F32))
               * jnp.dot(x, wu, preferred_element_type=F32))
        o_ref[...] = jnp.dot(hid.astype(BF16), wd, preferred_element_type=F32).astype(BF16)


def _expert_ffn(xs_a, xs_b, w_gate, w_up, w_down, layer):
    E, S, D = xs_a.shape
    xspec = pl.BlockSpec((None, S, D), lambda e: (e, 0, 0))
    out = jax.ShapeDtypeStruct((E, S, D), BF16)
    return pl.pallas_call(
        _ffn_kernel,
        out_shape=(out, out),
        grid=(E,),
        in_specs=[xspec, xspec,
                  pl.BlockSpec((None, None, D, D_FF), lambda e: (layer, e, 0, 0)),
                  pl.BlockSpec((None, None, D, D_FF), lambda e: (layer, e, 0, 0)),
                  pl.BlockSpec((None, None, D_FF, D), lambda e: (layer, e, 0, 0))],
        out_specs=(xspec, xspec),
        compiler_params=_cparams(1),
        name="expert_ffn",
    )(xs_a, xs_b, w_gate, w_up, w_down)


def _scatter_kernel(cap, eg, cond_base, cond_stride, final, ys_ref, slot_ref, aff_ref, x_ref, mod_ref, fg_ref,
                    o_ref):
    row = cond_base + cond_stride * pl.program_id(0)
    gate2 = mod_ref[pl.ds(row, 1), pl.ds(5 * D_MODEL, D_MODEL)]
    slot = slot_ref[...].astype(BF16)
    aff = aff_ref[...].astype(BF16)
    tt = slot.shape[0]
    K = eg * cap
    shift = cap.bit_length() - 1
    col_expert = lax.broadcasted_iota(jnp.int32, (LANES, K), 1) >> shift
    lane_expert = lax.broadcasted_iota(jnp.int32, (LANES, K), 0)
    col_slot = (lax.broadcasted_iota(jnp.int32, (tt, K), 1) & (cap - 1)).astype(F32)
    y = jnp.zeros((tt, D_MODEL), F32)
    for g in range(N_EXPERTS // eg):
        expand = jnp.where(lane_expert == col_expert + g * eg, 1.0, 0.0).astype(BF16)
        slot_x = jnp.dot(slot, expand, preferred_element_type=F32)
        aff_x = jnp.dot(aff, expand, preferred_element_type=F32)
        w = jnp.where(slot_x == col_slot, aff_x, 0.0).astype(BF16)
        ys = jnp.concatenate([ys_ref[g * eg + j] for j in range(eg)], axis=0)
        y = y + jnp.dot(w, ys, preferred_element_type=F32)
    x2 = x_ref[...] + gate2 * y
    if final:
        x2 = _rms(x2) * fg_ref[...]
    o_ref[...] = x2


def _scatter(ys, slot, aff, x1, mod, fg, cap, cond_base, cond_stride, final, layer):
    B, L, D = x1.shape
    tt = min(L, 1024)
    assert cap & (cap - 1) == 0
    eg = max(1, min(N_EXPERTS, GATHER_ROWS // cap))

    def rows(w):
        return pl.BlockSpec((None, tt, w), lambda b, i: (b, i, 0))

    return pl.pallas_call(
        functools.partial(_scatter_kernel, cap, eg, cond_base, cond_stride, final),
        out_shape=jax.ShapeDtypeStruct((B, L, D), F32),
        grid=(B, L // tt),
        in_specs=[pl.BlockSpec((N_EXPERTS, cap, D), lambda b, i: (0, b, 0)),
                  rows(LANES), rows(LANES), rows(D),
                  pl.BlockSpec((None, 8, N_MOD * D), lambda b, i: (layer, 0, 0)),
                  pl.BlockSpec((1, D), lambda b, i: (0, 0))],
        out_specs=rows(D),
        compiler_params=_cparams(2),
        name="expert_scatter",
    )(ys, slot, aff, x1, mod, fg)


def kernel(x_prompt, x_sample, c, cache_diff_k, cache_diff_v, state_hgrn, c_ctx, norm_g, final_norm_g,
           w_mod, b_mod, w_in, w_out, hgrn_lb_logits, hgrn_norm_g, diff_lambda, diff_subln_g,
           hy_conv_w, hy_conv_b, hy_filt_w1, hy_filt_b1, hy_filt_w2, hy_filt_b2, hy_filt_w3, hy_filt_freq,
           hy_skip, w_router, w_gate, w_up, w_down):
    B_p, L_p, D = x_prompt.shape
    B_s, L_s, _ = x_sample.shape
    cap_p = EC_CAPACITY_FACTOR * L_p // N_EXPERTS
    cap_s = EC_CAPACITY_FACTOR * L_s // N_EXPERTS
    assert B_p * cap_p == B_s * cap_s

    cond8 = jnp.zeros((8, D), F32).at[0].set(c_ctx.astype(F32)).at[1:1 + B_s].set(c.astype(F32))
    mod = _modulation(cond8, w_mod, b_mod)
    groups = (
        dict(cond=(0, 0), cap=cap_p, L=L_p),
        dict(cond=(1, 1), cap=cap_s, L=L_s),
    )
    filt = {L: _hyena_filters(L, hy_filt_w1, hy_filt_b1, hy_filt_w2, hy_filt_b2, hy_filt_w3, hy_filt_freq)
            for L in {L_p, L_s}}
    rope_tabs = _rope_tables(L_s)
    fg = final_norm_g.astype(F32).reshape(1, D)
    wr_pad = jnp.pad(w_router.astype(F32), ((0, 0), (0, 0), (0, LANES - N_EXPERTS)))
    caches = None
    w_in_bf = w_in.astype(BF16)
    w_out_bf = w_out.astype(BF16)

    xs = [x_prompt.astype(F32), x_sample.astype(F32)]
    new_s = []
    for l in range(DEPTH):
        ng = jnp.tile(hgrn_norm_g[l].astype(F32), HG_HEADS).reshape(1, HG_W)
        sg = diff_subln_g[l].astype(F32).reshape(1, DA_DV)
        g1 = norm_g[l, 0].astype(F32).reshape(1, D)
        g2 = norm_g[l, 1].astype(F32).reshape(1, D)
        x1s, slots, affs, gathered = [], [], [], []
        for gi, grp in enumerate(groups):
            x = xs[gi]
            cb, cs = grp["cond"]
            if gi == 0:
                z, qa, ka, va, kc, vc = _in_proj(x, mod, g1, w_in_bf, cb, cs, l, caches=caches,
                                                 cache_dtype=x_prompt.dtype)
                caches = (kc, vc)
                o_hg, s_fin = _hgrn(z, hgrn_lb_logits.astype(F32), ng, l)
                new_s.append(s_fin)
            else:
                z, qa, ka, va = _in_proj(x, mod, g1, w_in_bf, cb, cs, l, rope_tabs=rope_tabs)
                o_hg, _ = _hgrn(z, hgrn_lb_logits.astype(F32), ng, l, state=state_hgrn.astype(F32))
            if gi == 0:
                o_da = _attention(qa, ka, va, l, diff_lambda[l].astype(F32), sg)
            else:
                o_da = _attention(qa, ka, va, l, diff_lambda[l].astype(F32), sg,
                                  cache_diff_k.astype(F32), cache_diff_v.astype(F32))
            o_hy = _hyena(z, filt[grp["L"]], l, hy_conv_w[l].astype(F32), hy_conv_b[l].astype(F32),
                          hy_skip[l].astype(F32))
            x1, h2, aff = _out_proj(o_hg, o_da, o_hy, x, mod, g2, w_out_bf, wr_pad, cb, cs, l)
            slot, slot_t = _route(aff, grp["cap"])
            x1s.append(x1)
            slots.append(slot)
            affs.append(aff)
            gathered.append(_gather(slot_t, h2, grp["cap"]))
        ys = _expert_ffn(gathered[0], gathered[1], w_gate, w_up, w_down, l)
        for gi, grp in enumerate(groups):
            cb, cs = grp["cond"]
            xs[gi] = _scatter(ys[gi], slots[gi], affs[gi], x1s[gi], mod, fg, grp["cap"], cb, cs,
                              final=(l == DEPTH - 1), layer=l)

    dt = x_prompt.dtype
    return (xs[0].astype(dt), xs[1].astype(x_sample.dtype),
            caches[0], caches[1], jnp.stack(new_s, axis=1).astype(dt))
```
